```python
import math
import jax, jax.numpy as jnp
from jax import lax
import numpy as np

D_MODEL = 1024
BATCH = 8
SEQ = 2048
DEPTH = 1

CHUNK = 64
Q_BLOCK = 128
D_RNN = D_MODEL
RNN_BLOCKS = 16
RNN_BW = D_RNN // RNN_BLOCKS
CONV_W = 4
LRU_C = 8.0
N_DIFF_HEADS = 8
DIFF_DH = D_MODEL // (2 * N_DIFF_HEADS)
D_QK = N_DIFF_HEADS * 2 * DIFF_DH
D_ATTN = N_DIFF_HEADS * 2 * DIFF_DH
X_OFF = 0
Y_OFF = X_OFF + D_RNN
Q_OFF = Y_OFF + D_RNN
K_OFF = Q_OFF + D_QK
V_OFF = K_OFF + D_QK
G_OFF = V_OFF + D_ATTN
D_IN = G_OFF + 2 * D_MODEL
N_EXPERTS = 64
TOP_K = 8
N_GROUPS = 8
TOPK_GROUPS = 4
D_EXPERT = 256
D_SHARED = 256
ROUTED_SCALE = 2.5
TOK_BLOCK = 128
ALPHA = (2.0 * DEPTH) ** 0.25
BETA = (8.0 * DEPTH) ** -0.25
LN_EPS = 1e-5
RMS_EPS = 1e-5

kernel_name = 'hybrid_rglru_diffattn_moe_block'


def _ln0(x):
    xf = x.astype(jnp.float32)
    mu = jnp.mean(xf, axis=-1, keepdims=True)
    xc = xf - mu
    var = jnp.mean(jnp.square(xc), axis=-1, keepdims=True)
    return xc * lax.rsqrt(var + LN_EPS)


def _layer_norm(x, g, b):
    return _ln0(x) * g.astype(jnp.float32) + b.astype(jnp.float32)


def _modulate(x, shift, scale):
    return (_ln0(x) * (1.0 + scale[:, None, :]) + shift[:, None, :]).astype(x.dtype)


def _causal_depthwise_conv(x, w, b):
    y = lax.conv_general_dilated(
        x, w[:, None, :].astype(x.dtype), window_strides=(1,), padding=[(CONV_W - 1, 0)],
        dimension_numbers=('NWC', 'WIO', 'NWC'), feature_group_count=x.shape[-1])
    return y + b.astype(x.dtype)


def _rg_lru(x, w_a, b_a, w_i, b_i, lru_lambda):
    B, S, _ = x.shape
    xf = x.astype(jnp.float32)
    xb = xf.reshape(B, S, RNN_BLOCKS, RNN_BW)
    r = jax.nn.sigmoid(jnp.einsum('bsnd,nde->bsne', xb, w_a.astype(jnp.float32)).reshape(B, S, D_RNN) + b_a.astype(jnp.float32))
    i = jax.nn.sigmoid(jnp.einsum('bsnd,nde->bsne', xb, w_i.astype(jnp.float32)).reshape(B, S, D_RNN) + b_i.astype(jnp.float32))
    log_a = -LRU_C * r * jax.nn.softplus(-lru_lambda.astype(jnp.float32))
    a = jnp.exp(log_a)
    mult = jnp.sqrt(jnp.maximum(-jnp.expm1(2.0 * log_a), 0.0))
    u = xf * i * mult

    def combine(left, right):
        a_l, b_l = left
        a_r, b_r = right
        return a_l * a_r, a_r * b_l + b_r

    _, h = lax.associative_scan(combine, (a, u), axis=1)
    return h


def _diff_attention(q, k, v, lam, subln_g, lam_init):
    B, S = q.shape[0], q.shape[1]
    nb = S // Q_BLOCK
    scale = DIFF_DH ** -0.5
    qb = q.reshape(B, nb, Q_BLOCK, N_DIFF_HEADS, 2, DIFF_DH).transpose(1, 0, 3, 4, 2, 5)
    kt = k.transpose(0, 2, 3, 1, 4).astype(jnp.float32)
    vt = v.transpose(0, 2, 1, 3).astype(jnp.float32)
    key_chunk = jnp.arange(S) // CHUNK

    def block(args):
        q_blk, blk = args
        scores = jnp.einsum('bhcqd,bhckd->bhcqk', q_blk.astype(jnp.float32), kt) * scale
        q_chunk = (blk * Q_BLOCK + jnp.arange(Q_BLOCK)) // CHUNK
        mask = key_chunk[None, :] <= q_chunk[:, None]
        scores = jnp.where(mask, scores, -jnp.inf)
        p = jax.nn.softmax(scores, axis=-1)
        attn = p[:, :, 0] - lam * p[:, :, 1]
        return jnp.einsum('bhqk,bhkv->bhqv', attn, vt)

    o = lax.map(block, (qb, jnp.arange(nb)))
    o = o.transpose(1, 0, 3, 2, 4).reshape(B, S, N_DIFF_HEADS, 2 * DIFF_DH)
    o = o * lax.rsqrt(jnp.mean(jnp.square(o), axis=-1, keepdims=True) + RMS_EPS) * subln_g.astype(jnp.float32)
    o = o * (1.0 - lam_init)
    return o.reshape(B, S, D_ATTN)


def _token_mixer(h, w_in, conv_w, conv_b, lru_w_a, lru_b_a, lru_w_i, lru_b_i, lru_lambda,
                 lam_q1, lam_k1, lam_q2, lam_k2, subln_g, p_rnn, p_attn, w_out, lam_init):
    B, S, _ = h.shape
    proj = h @ w_in
    xr = proj[..., X_OFF:Y_OFF]
    yr = proj[..., Y_OFF:Q_OFF]
    q = proj[..., Q_OFF:K_OFF].reshape(B, S, N_DIFF_HEADS, 2, DIFF_DH)
    k = proj[..., K_OFF:V_OFF].reshape(B, S, N_DIFF_HEADS, 2, DIFF_DH)
    v = proj[..., V_OFF:G_OFF].reshape(B, S, N_DIFF_HEADS, 2 * DIFF_DH)
    gates = jax.nn.sigmoid(proj[..., G_OFF:D_IN].astype(jnp.float32))
    g_rnn, g_attn = gates[..., :D_MODEL], gates[..., D_MODEL:]
    xr = _causal_depthwise_conv(xr, conv_w, conv_b)
    hr = _rg_lru(xr, lru_w_a, lru_b_a, lru_w_i, lru_b_i, lru_lambda)
    ya = (hr * jax.nn.gelu(yr.astype(jnp.float32))).astype(h.dtype)
    branch_a = (ya @ p_rnn).astype(jnp.float32)
    lam = (jnp.exp(jnp.sum(lam_q1.astype(jnp.float32) * lam_k1.astype(jnp.float32)))
           - jnp.exp(jnp.sum(lam_q2.astype(jnp.float32) * lam_k2.astype(jnp.float32))) + lam_init)
    yb = _diff_attention(q, k, v, lam, subln_g, lam_init).astype(h.dtype)
    branch_b = (yb @ p_attn).astype(jnp.float32)
    merged = (g_rnn * branch_a + g_attn * branch_b).astype(h.dtype)
    return (merged @ w_out).astype(jnp.float32)


def _moe(u, w_router, router_bias, we_gate, we_up, we_down, ws_gate, ws_up, ws_down):
    B, S, D = u.shape
    n = B * S
    t = u.reshape(n, D)
    s = jax.nn.sigmoid((t @ w_router).astype(jnp.float32))
    sel = s + router_bias.astype(jnp.float32)
    grp = sel.reshape(n, N_GROUPS, N_EXPERTS // N_GROUPS)
    grp_score = jnp.sum(lax.top_k(grp, 2)[0], axis=-1)
    _, gidx = lax.top_k(grp_score, TOPK_GROUPS)
    gmask = jnp.sum(jax.nn.one_hot(gidx, N_GROUPS, dtype=jnp.float32), axis=1) > 0
    emask = jnp.repeat(gmask, N_EXPERTS // N_GROUPS, axis=1)
    sel = jnp.where(emask, sel, -jnp.inf)
    _, eidx = lax.top_k(sel, TOP_K)
    w = jnp.take_along_axis(s, eidx, axis=-1)
    w = w / jnp.sum(w, axis=-1, keepdims=True) * ROUTED_SCALE
    gate = jnp.sum(jax.nn.one_hot(eidx, N_EXPERTS, dtype=jnp.float32) * w[..., None], axis=1)
    tb = t.reshape(n // TOK_BLOCK, TOK_BLOCK, D)
    gb = gate.reshape(n // TOK_BLOCK, TOK_BLOCK, N_EXPERTS)

    def expert_block(args):
        tk, gk = args
        hg = jnp.einsum('nd,edf->nef', tk, we_gate).astype(jnp.float32)
        hu = jnp.einsum('nd,edf->nef', tk, we_up).astype(jnp.float32)
        act = (jax.nn.silu(hg) * hu * gk[..., None]).astype(tk.dtype)
        return jnp.einsum('nef,efd->nd', act, we_down).astype(jnp.float32)

    routed = lax.map(expert_block, (tb, gb)).reshape(n, D)
    shared = ((jax.nn.silu((t @ ws_gate).astype(jnp.float32)) * (t @ ws_up).astype(jnp.float32)).astype(t.dtype) @ ws_down).astype(jnp.float32)
    return (routed + shared).reshape(B, S, D)


def setup_inputs(seed: int = 0) -> dict:
    key = jax.random.key(seed)
    ks = jax.random.split(key, 32)

    def nrm(k, shape, scale):
        return jax.random.normal(k, shape, jnp.float32) * scale

    L = DEPTH
    x = nrm(ks[0], (BATCH, SEQ, D_MODEL), 1.0)
    c = nrm(ks[1], (BATCH, D_MODEL), 1.0)
    w_ada = nrm(ks[2], (L, D_MODEL, 6 * D_MODEL), D_MODEL ** -0.5)
    b_ada = nrm(ks[3], (L, 6 * D_MODEL), 0.02)
    w_in = nrm(ks[4], (L, D_MODEL, D_IN), D_MODEL ** -0.5)
    w_in = w_in.at[:, :, V_OFF:G_OFF].multiply(BETA)
    conv_w = nrm(ks[5], (L, CONV_W, D_RNN), CONV_W ** -0.5)
    conv_b = nrm(ks[6], (L, D_RNN), 0.02)
    lru_w_a = nrm(ks[7], (L, RNN_BLOCKS, RNN_BW, RNN_BW), RNN_BW ** -0.5)
    lru_b_a = nrm(ks[8], (L, D_RNN), 0.02)
    lru_w_i = nrm(ks[9], (L, RNN_BLOCKS, RNN_BW, RNN_BW), RNN_BW ** -0.5)
    lru_b_i = nrm(ks[10], (L, D_RNN), 0.02)
    a8 = jax.random.uniform(ks[11], (L, D_RNN), jnp.float32, 0.9, 0.999)
    a0 = a8 ** (1.0 / LRU_C)
    lru_lambda = jnp.log(a0) - jnp.log1p(-a0)
    lam_q1 = nrm(ks[12], (L, DIFF_DH), 0.1)
    lam_k1 = nrm(ks[13], (L, DIFF_DH), 0.1)
    lam_q2 = nrm(ks[14], (L, DIFF_DH), 0.1)
    lam_k2 = nrm(ks[15], (L, DIFF_DH), 0.1)
    subln_g = 1.0 + nrm(ks[16], (L, 2 * DIFF_DH), 0.02)
    p_rnn = nrm(ks[17], (L, D_RNN, D_MODEL), D_RNN ** -0.5)
    p_attn = nrm(ks[18], (L, D_ATTN, D_MODEL), D_ATTN ** -0.5)
    w_out = nrm(ks[19], (L, D_MODEL, D_MODEL), D_MODEL ** -0.5 * BETA)
    ln1_g = 1.0 + nrm(ks[20], (L, D_MODEL), 0.02)
    ln1_b = nrm(ks[21], (L, D_MODEL), 0.02)
    w_router = nrm(ks[22], (L, D_MODEL, N_EXPERTS), D_MODEL ** -0.5)
    router_bias = nrm(ks[23], (L, N_EXPERTS), 0.01)
    we_gate = nrm(ks[24], (L, N_EXPERTS, D_MODEL, D_EXPERT), D_MODEL ** -0.5)
    we_up = nrm(ks[25], (L, N_EXPERTS, D_MODEL, D_EXPERT), D_MODEL ** -0.5)
    we_down = nrm(ks[26], (L, N_EXPERTS, D_EXPERT, D_MODEL), D_EXPERT ** -0.5 * BETA)
    ws_gate = nrm(ks[27], (L, D_MODEL, D_SHARED), D_MODEL ** -0.5)
    ws_up = nrm(ks[28], (L, D_MODEL, D_SHARED), D_MODEL ** -0.5)
    ws_down = nrm(ks[29], (L, D_SHARED, D_MODEL), D_SHARED ** -0.5 * BETA)
    ln2_g = 1.0 + nrm(ks[30], (L, D_MODEL), 0.02)
    ln2_b = nrm(ks[31], (L, D_MODEL), 0.02)
    return {'x': x, 'c': c, 'w_ada': w_ada, 'b_ada': b_ada, 'w_in': w_in, 'conv_w': conv_w, 'conv_b': conv_b,
            'lru_w_a': lru_w_a, 'lru_b_a': lru_b_a, 'lru_w_i': lru_w_i, 'lru_b_i': lru_b_i, 'lru_lambda': lru_lambda,
            'lam_q1': lam_q1, 'lam_k1': lam_k1, 'lam_q2': lam_q2, 'lam_k2': lam_k2, 'subln_g': subln_g,
            'p_rnn': p_rnn, 'p_attn': p_attn, 'w_out': w_out, 'ln1_g': ln1_g, 'ln1_b': ln1_b,
            'w_router': w_router, 'router_bias': router_bias, 'we_gate': we_gate, 'we_up': we_up, 'we_down': we_down,
            'ws_gate': ws_gate, 'ws_up': ws_up, 'ws_down': ws_down, 'ln2_g': ln2_g, 'ln2_b': ln2_b}


def reference(x, c, w_ada, b_ada, w_in, conv_w, conv_b, lru_w_a, lru_b_a, lru_w_i, lru_b_i, lru_lambda,
              lam_q1, lam_k1, lam_q2, lam_k2, subln_g, p_rnn, p_attn, w_out, ln1_g, ln1_b,
              w_router, router_bias, we_gate, we_up, we_down, ws_gate, ws_up, ws_down, ln2_g, ln2_b):
    dtype = x.dtype
    cond = jax.nn.silu(c.astype(jnp.float32))
    for l in range(DEPTH):
        lam_init = 0.8 - 0.6 * math.exp(-0.3 * l)
        ada = cond @ w_ada[l].astype(jnp.float32) + b_ada[l].astype(jnp.float32)
        sh1, sc1, g1, sh2, sc2, g2 = jnp.split(ada, 6, axis=-1)
        h = _modulate(x, sh1, sc1)
        mix = _token_mixer(h, w_in[l], conv_w[l], conv_b[l], lru_w_a[l], lru_b_a[l], lru_w_i[l], lru_b_i[l],
                           lru_lambda[l], lam_q1[l], lam_k1[l], lam_q2[l], lam_k2[l], subln_g[l],
                           p_rnn[l], p_attn[l], w_out[l], lam_init)
        x = _layer_norm(ALPHA * x.astype(jnp.float32) + g1[:, None, :] * mix, ln1_g[l], ln1_b[l]).astype(dtype)
        u = _modulate(x, sh2, sc2)
        ffn = _moe(u, w_router[l], router_bias[l], we_gate[l], we_up[l], we_down[l], ws_gate[l], ws_up[l], ws_down[l])
        x = _layer_norm(ALPHA * x.astype(jnp.float32) + g2[:, None, :] * ffn, ln2_g[l], ln2_b[l]).astype(dtype)
    return x
```

```python
import functools
import math

import jax
import jax.numpy as jnp
from jax import lax
from jax.experimental import pallas as pl
from jax.experimental.pallas import tpu as pltpu

F32 = jnp.float32
BF16 = jnp.bfloat16

D_MODEL = 1024
CHUNK = 64
RNN_BLOCKS = 16
CONV_W = 4
LRU_C = 8.0
N_HEADS = 8
DIFF_DH = 64
HEAD_W = 2 * DIFF_DH
N_EXPERTS = 64
TOP_K = 8
N_GROUPS = 8
GROUP_SIZE = N_EXPERTS // N_GROUPS
TOPK_GROUPS = 4
D_EXPERT = 256
ROUTED_SCALE = 2.5
DEPTH = 1
ALPHA = (2.0 * DEPTH) ** 0.25
LN_EPS = 1e-5
RMS_EPS = 1e-5
LAM_INIT = 0.8 - 0.6 * math.exp(-0.3 * 0)

COL_X, COL_Y, COL_Q, COL_K, COL_V, COL_GR, COL_GA = range(7)
N_COLS = 7

V7X_VMEM_LIMIT = 56 * 1024 * 1024
MXU_TILE = 256


def _ln0(x):
    mu = jnp.mean(x, axis=-1, keepdims=True)
    xc = x - mu
    var = jnp.mean(xc * xc, axis=-1, keepdims=True)
    return xc * lax.rsqrt(var + LN_EPS)


def _silu(x):
    return x * jax.nn.sigmoid(x)


def _params(sem):
    return pltpu.CompilerParams(dimension_semantics=sem, vmem_limit_bytes=V7X_VMEM_LIMIT)


def _ada_kernel(c_ref, w_ref, b_ref, o_ref):
    cond = _silu(c_ref[...])
    o_ref[...] = jnp.dot(cond, w_ref[...], preferred_element_type=F32,
                         precision=lax.Precision.HIGHEST) + b_ref[...]


def _ada(c, w, b):
    bsz, d = c.shape
    n_out = w.shape[1]
    return pl.pallas_call(
        _ada_kernel,
        grid=(n_out // d,),
        in_specs=[pl.BlockSpec((bsz, d), lambda j: (0, 0)),
                  pl.BlockSpec((d, d), lambda j: (0, j)),
                  pl.BlockSpec((1, d), lambda j: (0, j))],
        out_specs=pl.BlockSpec((bsz, d), lambda j: (0, j)),
        out_shape=jax.ShapeDtypeStruct((bsz, n_out), F32),
        compiler_params=_params(("parallel",)),
        name="ada",
    )(c, w, b.reshape(1, n_out))


def _inproj_kernel(x_ref, sh_ref, sc_ref, w_ref, o_ref, h_scr):
    j = pl.program_id(1)

    @pl.when(j == 0)
    def _():
        h = _ln0(x_ref[...]) * (1.0 + sc_ref[0]) + sh_ref[0]
        h_scr[...] = h.astype(BF16)

    acc = jnp.dot(h_scr[...], w_ref[0], preferred_element_type=F32)

    @pl.when(j < COL_GR)
    def _():
        o_ref[...] = acc.astype(BF16)

    @pl.when(j >= COL_GR)
    def _():
        o_ref[...] = jax.nn.sigmoid(acc).astype(BF16)


def _inproj(x2, sh, sc, w3, seq, tm):
    n, d = x2.shape
    per_b = seq // tm
    return pl.pallas_call(
        _inproj_kernel,
        grid=(n // tm, N_COLS),
        in_specs=[pl.BlockSpec((tm, d), lambda i, j: (i, 0)),
                  pl.BlockSpec((1, 1, d), lambda i, j: (i // per_b, 0, 0)),
                  pl.BlockSpec((1, 1, d), lambda i, j: (i // per_b, 0, 0)),
                  pl.BlockSpec((1, d, d), lambda i, j: (j, 0, 0))],
        out_specs=pl.BlockSpec((tm, d), lambda i, j: (i, j)),
        out_shape=jax.ShapeDtypeStruct((n, N_COLS * d), BF16),
        scratch_shapes=[pltpu.VMEM((tm, d), BF16)],
        compiler_params=_params(("parallel", "arbitrary")),
        name="inproj",
    )(x2, sh, sc, w3)


def _rnn_kernel(xr_ref, yr_ref, g_ref, cw_ref, cb_ref, wa_ref, ba_ref, wi_ref, bi_ref,
                lam_ref, p_ref, o_ref, xe_scr, h_scr):
    t = pl.program_id(1)
    tt, d = xr_ref.shape
    pad = 8

    @pl.when(t == 0)
    def _():
        xe_scr[0:pad, :] = jnp.zeros((pad, d), F32)
        h_scr[...] = jnp.zeros_like(h_scr)

    x = xr_ref[...].astype(F32)
    xe_scr[pad:pad + tt, :] = x
    cw = cw_ref[...]
    xc = cb_ref[...] + cw[CONV_W - 1:CONV_W, :] * x
    for k in range(1, CONV_W):
        xc = xc + cw[CONV_W - 1 - k:CONV_W - k, :] * xe_scr[pad - k:pad - k + tt, :]
    xe_scr[0:pad, :] = x[tt - pad:tt, :]

    xcb = xc.astype(BF16)
    n_g = d // MXU_TILE
    ra = jnp.concatenate(
        [jnp.dot(xcb[:, g * MXU_TILE:(g + 1) * MXU_TILE], wa_ref[g], preferred_element_type=F32)
         for g in range(n_g)], axis=-1)
    ri = jnp.concatenate(
        [jnp.dot(xcb[:, g * MXU_TILE:(g + 1) * MXU_TILE], wi_ref[g], preferred_element_type=F32)
         for g in range(n_g)], axis=-1)
    r = jax.nn.sigmoid(ra + ba_ref[...])
    gi = jax.nn.sigmoid(ri + bi_ref[...])
    z = -lam_ref[...]
    softplus = jnp.maximum(z, 0.0) + jnp.log1p(jnp.exp(-jnp.abs(z)))
    log_a = (-LRU_C) * r * softplus
    a = jnp.exp(log_a)
    mult = jnp.sqrt(jnp.maximum(1.0 - a * a, 0.0))
    u = xc * gi * mult

    row = lax.broadcasted_iota(jnp.int32, (tt, d), 0)
    step = 1
    while step < tt:
        valid = row >= step
        a_prev = pltpu.roll(a, step, 0)
        u_prev = pltpu.roll(u, step, 0)
        u = jnp.where(valid, a * u_prev + u, u)
        a = jnp.where(valid, a * a_prev, a)
        step *= 2
    h = a * h_scr[...] + u
    h_scr[...] = h[tt - 1:tt, :]

    ya = (h * jax.nn.gelu(yr_ref[...].astype(F32))).astype(BF16)
    branch = jnp.dot(ya, p_ref[...], preferred_element_type=F32)
    o_ref[...] = (g_ref[...].astype(F32) * branch).astype(BF16)


def _rnn(proj, cw, cb, wa, ba, wi, bi, lam, p_rnn, bsz, seq, tt):
    n = proj.shape[0]
    d = D_MODEL
    per_b = seq // tt
    row = lambda b, t: b * per_b + t
    vec = pl.BlockSpec((1, d), lambda b, t: (0, 0))
    gate_w = pl.BlockSpec(wa.shape, lambda b, t: (0, 0, 0))
    return pl.pallas_call(
        _rnn_kernel,
        grid=(bsz, per_b),
        in_specs=[pl.BlockSpec((tt, d), lambda b, t: (row(b, t), COL_X)),
                  pl.BlockSpec((tt, d), lambda b, t: (row(b, t), COL_Y)),
                  pl.BlockSpec((tt, d), lambda b, t: (row(b, t), COL_GR)),
                  pl.BlockSpec((CONV_W, d), lambda b, t: (0, 0)),
                  vec, gate_w, vec, gate_w, vec, vec,
                  pl.BlockSpec((d, d), lambda b, t: (0, 0))],
        out_specs=pl.BlockSpec((tt, d), lambda b, t: (row(b, t), 0)),
        out_shape=jax.ShapeDtypeStruct((n, d), BF16),
        scratch_shapes=[pltpu.VMEM((tt + 8, d), F32), pltpu.VMEM((1, d), F32)],
        compiler_params=_params(("parallel", "arbitrary")),
        name="rnn",
    )(proj, proj, proj, cw, cb, wa, ba, wi, bi, lam, p_rnn)


def _attn_kernel(q_ref, k_ref, v_ref, lq1_ref, lk1_ref, lq2_ref, lk2_ref, sg_ref, o_ref):
    qi = pl.program_id(2)
    tq = q_ref.shape[0]
    seq = k_ref.shape[0]
    lam = (jnp.exp(jnp.sum(lq1_ref[...] * lk1_ref[...], axis=-1, keepdims=True))
           - jnp.exp(jnp.sum(lq2_ref[...] * lk2_ref[...], axis=-1, keepdims=True)) + LAM_INIT)

    q = q_ref[...]
    k = k_ref[...]
    lane = lax.broadcasted_iota(jnp.int32, q.shape, 1)
    zero = jnp.zeros_like(q)
    q1 = jnp.where(lane < DIFF_DH, q, zero)
    q2 = jnp.where(lane >= DIFF_DH, q, zero)
    nt = (((1,), (1,)), ((), ()))
    scale = DIFF_DH ** -0.5
    s1 = lax.dot_general(q1, k, nt, preferred_element_type=F32) * scale
    s2 = lax.dot_general(q2, k, nt, preferred_element_type=F32) * scale

    q_chunk = (qi * tq + lax.broadcasted_iota(jnp.int32, (tq, seq), 0)) // CHUNK
    k_chunk = lax.broadcasted_iota(jnp.int32, (tq, seq), 1) // CHUNK
    mask = k_chunk <= q_chunk

    def softmax(s):
        s = jnp.where(mask, s, -jnp.inf)
        e = jnp.exp(s - jnp.max(s, axis=-1, keepdims=True))
        return e / jnp.sum(e, axis=-1, keepdims=True)

    attn = softmax(s1) - lam * softmax(s2)
    o = jnp.dot(attn.astype(BF16), v_ref[...], preferred_element_type=F32)
    o = o * lax.rsqrt(jnp.mean(o * o, axis=-1, keepdims=True) + RMS_EPS) * sg_ref[...]
    o_ref[...] = (o * (1.0 - LAM_INIT)).astype(BF16)


def _attn(proj, lq1, lk1, lq2, lk2, sg, bsz, seq, tq):
    n = proj.shape[0]
    per_b = seq // tq
    hb = D_MODEL // HEAD_W
    lam_spec = pl.BlockSpec((1, DIFF_DH), lambda b, h, i: (0, 0))
    return pl.pallas_call(
        _attn_kernel,
        grid=(bsz, N_HEADS, per_b),
        in_specs=[pl.BlockSpec((tq, HEAD_W), lambda b, h, i: (b * per_b + i, COL_Q * hb + h)),
                  pl.BlockSpec((seq, HEAD_W), lambda b, h, i: (b, COL_K * hb + h)),
                  pl.BlockSpec((seq, HEAD_W), lambda b, h, i: (b, COL_V * hb + h)),
                  lam_spec, lam_spec, lam_spec, lam_spec,
                  pl.BlockSpec((1, HEAD_W), lambda b, h, i: (0, 0))],
        out_specs=pl.BlockSpec((tq, HEAD_W), lambda b, h, i: (b * per_b + i, h)),
        out_shape=jax.ShapeDtypeStruct((n, D_MODEL), BF16),
        compiler_params=_params(("parallel", "parallel", "arbitrary")),
        name="attn",
    )(proj, proj, proj, lq1, lk1, lq2, lk2, sg)


def _postmix_kernel(yb_ref, ma_ref, ga_ref, x_ref, g1_ref, sh_ref, sc_ref, lg_ref, lb_ref,
                    pa_ref, wo_ref, wr_ref, x1_ref, u_ref, lt_ref):
    branch_b = jnp.dot(yb_ref[...], pa_ref[...], preferred_element_type=F32)
    merged = ma_ref[...].astype(F32) + ga_ref[...].astype(F32) * branch_b
    mix = jnp.dot(merged.astype(BF16), wo_ref[...], preferred_element_type=F32)
    x1 = _ln0(ALPHA * x_ref[...] + g1_ref[0] * mix) * lg_ref[...] + lb_ref[...]
    x1_ref[...] = x1
    u = _ln0(x1) * (1.0 + sc_ref[0]) + sh_ref[0]
    u_ref[...] = u.astype(BF16)
    nt = (((1,), (1,)), ((), ()))
    lt_ref[...] = lax.dot_general(wr_ref[...], u, nt, preferred_element_type=F32,
                                  precision=lax.Precision.HIGHEST)


def _postmix(yb, ma, proj, x2, g1, sh2, sc2, ln_g, ln_b, p_attn, w_out, wr_t, seq, tm):
    n, d = x2.shape
    per_b = seq // tm
    row = pl.BlockSpec((tm, d), lambda i: (i, 0))
    mod = pl.BlockSpec((1, 1, d), lambda i: (i // per_b, 0, 0))
    vec = pl.BlockSpec((1, d), lambda i: (0, 0))
    mat = pl.BlockSpec((d, d), lambda i: (0, 0))
    return pl.pallas_call(
        _postmix_kernel,
        grid=(n // tm,),
        in_specs=[row, row, pl.BlockSpec((tm, d), lambda i: (i, COL_GA)), row,
                  mod, mod, mod, vec, vec, mat, mat,
                  pl.BlockSpec((N_EXPERTS, d), lambda i: (0, 0))],
        out_specs=[row, row, pl.BlockSpec((N_EXPERTS, tm), lambda i: (0, i))],
        out_shape=[jax.ShapeDtypeStruct((n, d), F32),
                   jax.ShapeDtypeStruct((n, d), BF16),
                   jax.ShapeDtypeStruct((N_EXPERTS, n), F32)],
        compiler_params=_params(("parallel",)),
        name="postmix",
    )(yb, ma, proj, x2, g1, sh2, sc2, ln_g, ln_b, p_attn, w_out, wr_t)


def _route_kernel(lt_ref, bias_ref, gate_ref):
    tn = lt_ref.shape[1]
    shape3 = (N_GROUPS, GROUP_SIZE, tn)
    neg = -jnp.inf
    s = jax.nn.sigmoid(lt_ref[...])
    sel = (s + bias_ref[...]).reshape(shape3)
    s = s.reshape(shape3)
    j_idx = lax.broadcasted_iota(jnp.int32, shape3, 1).astype(F32)
    g_idx = lax.broadcasted_iota(jnp.int32, shape3, 0).astype(F32)
    e_idx = g_idx * GROUP_SIZE + j_idx

    m1 = jnp.max(sel, axis=1, keepdims=True)
    first = jnp.min(jnp.where(sel == m1, j_idx, float(GROUP_SIZE)), axis=1, keepdims=True)
    m2 = jnp.max(jnp.where(j_idx == first, neg, sel), axis=1, keepdims=True)
    gscore = m1 + m2

    gi = lax.broadcasted_iota(jnp.int32, gscore.shape, 0).astype(F32)
    gkeep = jnp.zeros(gscore.shape, jnp.bool_)
    for _ in range(TOPK_GROUPS):
        m = jnp.max(gscore, axis=0, keepdims=True)
        pick = jnp.min(jnp.where(gscore == m, gi, float(N_GROUPS)), axis=0, keepdims=True)
        hit = gi == pick
        gkeep = jnp.logical_or(gkeep, hit)
        gscore = jnp.where(hit, neg, gscore)
    sel = jnp.where(gkeep, sel, neg)

    picked = jnp.zeros(shape3, F32)
    for _ in range(TOP_K):
        m = jnp.max(jnp.max(sel, axis=1, keepdims=True), axis=0, keepdims=True)
        cand = jnp.where(sel == m, e_idx, float(N_EXPERTS))
        pick = jnp.min(jnp.min(cand, axis=1, keepdims=True), axis=0, keepdims=True)
        hit = e_idx == pick
        picked = jnp.where(hit, s, picked)
        sel = jnp.where(hit, neg, sel)
    total = jnp.sum(jnp.sum(picked, axis=1, keepdims=True), axis=0, keepdims=True)
    gate = picked / total * ROUTED_SCALE
    gate_ref[...] = gate.reshape(N_EXPERTS, tn)


def _route(logits_t, bias, tn):
    e, n = logits_t.shape
    return pl.pallas_call(
        _route_kernel,
        grid=(n // tn,),
        in_specs=[pl.BlockSpec((e, tn), lambda i: (0, i)),
                  pl.BlockSpec((e, 1), lambda i: (0, 0))],
        out_specs=pl.BlockSpec((e, tn), lambda i: (0, i)),
        out_shape=jax.ShapeDtypeStruct((e, n), F32),
        compiler_params=_params(("parallel",)),
        name="route",
    )(logits_t, bias.reshape(e, 1))


def _moe_kernel(u_ref, gate_ref, x1_ref, g2_ref, wg_ref, wu_ref, wd_ref, sg_ref, su_ref, sd_ref,
                lg_ref, lb_ref, o_ref, acc_scr):
    e = pl.program_id(1)
    u = u_ref[...]

    @pl.when(e == 0)
    def _():
        hs = _silu(jnp.dot(u, sg_ref[...], preferred_element_type=F32)) * jnp.dot(
            u, su_ref[...], preferred_element_type=F32)
        acc_scr[...] = jnp.dot(hs.astype(BF16), sd_ref[...], preferred_element_type=F32)

    gate = gate_ref[...]
    lane = lax.broadcasted_iota(jnp.int32, gate.shape, 1)
    gk = jnp.sum(jnp.where(lane == e, gate, 0.0), axis=-1, keepdims=True)
    hg = jnp.dot(u, wg_ref[0], preferred_element_type=F32)
    hu = jnp.dot(u, wu_ref[0], preferred_element_type=F32)
    act = (_silu(hg) * hu * gk).astype(BF16)
    acc_scr[...] += jnp.dot(act, wd_ref[0], preferred_element_type=F32)

    @pl.when(e == pl.num_programs(1) - 1)
    def _():
        z = ALPHA * x1_ref[...] + g2_ref[0] * acc_scr[...]
        o_ref[...] = _ln0(z) * lg_ref[...] + lb_ref[...]


def _moe(u, gate, x1, g2, wg, wu, wd, sg, su, sd, ln_g, ln_b, seq, tm):
    n, d = x1.shape
    n_e, _, f = wg.shape
    per_b = seq // tm
    row = pl.BlockSpec((tm, d), lambda i, e: (i, 0))
    vec = pl.BlockSpec((1, d), lambda i, e: (0, 0))
    return pl.pallas_call(
        _moe_kernel,
        grid=(n // tm, n_e),
        in_specs=[row, pl.BlockSpec((tm, n_e), lambda i, e: (i, 0)), row,
                  pl.BlockSpec((1, 1, d), lambda i, e: (i // per_b, 0, 0)),
                  pl.BlockSpec((1, d, f), lambda i, e: (e, 0, 0)),
                  pl.BlockSpec((1, d, f), lambda i, e: (e, 0, 0)),
                  pl.BlockSpec((1, f, d), lambda i, e: (e, 0, 0)),
                  pl.BlockSpec(sg.shape, lambda i, e: (0, 0)),
                  pl.BlockSpec(su.shape, lambda i, e: (0, 0)),
                  pl.BlockSpec(sd.shape, lambda i, e: (0, 0)),
                  vec, vec],
        out_specs=row,
        out_shape=jax.ShapeDtypeStruct((n, d), F32),
        scratch_shapes=[pltpu.VMEM((tm, d), F32)],
        compiler_params=_params(("parallel", "arbitrary")),
        name="moe",
    )(u, gate, x1, g2, wg, wu, wd, sg, su, sd, ln_g, ln_b)


def _block_diag_tiles(w):
    nb, bw, _ = w.shape
    per = MXU_TILE // bw
    w = w.reshape(nb // per, per, bw, bw)
    eye = jnp.eye(per, dtype=w.dtype)
    tiles = jnp.einsum("gpij,pq->gpiqj", w, eye)
    return tiles.reshape(nb // per, MXU_TILE, MXU_TILE)


def kernel(x, c, w_ada, b_ada, w_in, conv_w, conv_b, lru_w_a, lru_b_a, lru_w_i, lru_b_i, lru_lambda,
           lam_q1, lam_k1, lam_q2, lam_k2, subln_g, p_rnn, p_attn, w_out, ln1_g, ln1_b,
           w_router, router_bias, we_gate, we_up, we_down, ws_gate, ws_up, ws_down, ln2_g, ln2_b):
    bsz, seq, d = x.shape
    n = bsz * seq
    x2 = x.reshape(n, d)
    l = 0

    ada = _ada(c, w_ada[l], b_ada[l])
    sh1, sc1, g1, sh2, sc2, g2 = [ada[:, i * d:(i + 1) * d].reshape(bsz, 1, d) for i in range(6)]

    w3 = w_in[l].astype(BF16).reshape(d, N_COLS, d).transpose(1, 0, 2)
    proj = _inproj(x2, sh1, sc1, w3, seq, tm=1024)

    ma = _rnn(proj, conv_w[l], conv_b[l].reshape(1, d),
              _block_diag_tiles(lru_w_a[l]).astype(BF16), lru_b_a[l].reshape(1, d),
              _block_diag_tiles(lru_w_i[l]).astype(BF16), lru_b_i[l].reshape(1, d),
              lru_lambda[l].reshape(1, d), p_rnn[l].astype(BF16), bsz, seq, tt=256)

    yb = _attn(proj, lam_q1[l].reshape(1, -1), lam_k1[l].reshape(1, -1),
               lam_q2[l].reshape(1, -1), lam_k2[l].reshape(1, -1),
               subln_g[l].reshape(1, -1), bsz, seq, tq=256)

    x1, u, logits_t = _postmix(yb, ma, proj, x2, g1, sh2, sc2, ln1_g[l].reshape(1, d),
                               ln1_b[l].reshape(1, d), p_attn[l].astype(BF16),
                               w_out[l].astype(BF16), w_router[l].T, seq, tm=512)

    gate = _route(logits_t, router_bias[l], tn=1024).T

    out = _moe(u, gate, x1, g2, we_gate[l].astype(BF16), we_up[l].astype(BF16),
               we_down[l].astype(BF16), ws_gate[l].astype(BF16), ws_up[l].astype(BF16),
               ws_down[l].astype(BF16), ln2_g[l].reshape(1, d), ln2_b[l].reshape(1, d),
               seq, tm=1024)
    return out.reshape(bsz, seq, d)
```

```python
import functools
import math

import jax
import jax.numpy as jnp
from jax import lax
from jax.experimental import pallas as pl
from jax.experimental.pallas import tpu as pltpu

F32 = jnp.float32
BF16 = jnp.bfloat16

D_MODEL = 1024
CHUNK = 64
RNN_BLOCKS = 16
CONV_W = 4
LRU_C = 8.0
N_HEADS = 8
DIFF_DH = 64
HEAD_W = 2 * DIFF_DH
N_EXPERTS = 64
TOP_K = 8
N_GROUPS = 8
GROUP_SIZE = N_EXPERTS // N_GROUPS
TOPK_GROUPS = 4
D_EXPERT = 256
ROUTED_SCALE = 2.5
DEPTH = 1
ALPHA = (2.0 * DEPTH) ** 0.25
LN_EPS = 1e-5
RMS_EPS = 1e-5
LAM_INIT = 0.8 - 0.6 * math.exp(-0.3 * 0)

COL_X, COL_Y, COL_Q, COL_K, COL_V, COL_GR, COL_GA = range(7)
N_COLS = 7

V7X_VMEM_LIMIT = 56 * 1024 * 1024
MXU_TILE = 256


def _ln0(x):
    mu = jnp.mean(x, axis=-1, keepdims=True)
    xc = x - mu
    var = jnp.mean(xc * xc, axis=-1, keepdims=True)
    return xc * lax.rsqrt(var + LN_EPS)


def _silu(x):
    return x * jax.nn.sigmoid(x)


def _params(sem):
    return pltpu.CompilerParams(dimension_semantics=sem, vmem_limit_bytes=V7X_VMEM_LIMIT)


def _ada_kernel(c_ref, w_ref, b_ref, o_ref):
    cond = _silu(c_ref[...])
    o_ref[...] = jnp.dot(cond, w_ref[...], preferred_element_type=F32,
                         precision=lax.Precision.HIGHEST) + b_ref[...]


def _ada(c, w, b):
    bsz, d = c.shape
    n_out = w.shape[1]
    return pl.pallas_call(
        _ada_kernel,
        grid=(n_out // d,),
        in_specs=[pl.BlockSpec((bsz, d), lambda j: (0, 0)),
                  pl.BlockSpec((d, d), lambda j: (0, j)),
                  pl.BlockSpec((1, d), lambda j: (0, j))],
        out_specs=pl.BlockSpec((bsz, d), lambda j: (0, j)),
        out_shape=jax.ShapeDtypeStruct((bsz, n_out), F32),
        compiler_params=_params(("parallel",)),
        name="ada",
    )(c, w, b.reshape(1, n_out))


def _inproj_kernel(x_ref, sh_ref, sc_ref, w_ref, o_ref, h_scr):
    j = pl.program_id(1)

    @pl.when(j == 0)
    def _():
        h = _ln0(x_ref[...]) * (1.0 + sc_ref[0]) + sh_ref[0]
        h_scr[...] = h.astype(BF16)

    acc = jnp.dot(h_scr[...], w_ref[0], preferred_element_type=F32)
    o_ref[...] = jnp.where(j >= COL_GR, jax.nn.sigmoid(acc), acc).astype(BF16)


def _inproj(x2, sh, sc, w3, seq, tm):
    n, d = x2.shape
    per_b = seq // tm
    return pl.pallas_call(
        _inproj_kernel,
        grid=(n // tm, N_COLS),
        in_specs=[pl.BlockSpec((tm, d), lambda i, j: (i, 0)),
                  pl.BlockSpec((1, 1, d), lambda i, j: (i // per_b, 0, 0)),
                  pl.BlockSpec((1, 1, d), lambda i, j: (i // per_b, 0, 0)),
                  pl.BlockSpec((1, d, d), lambda i, j: (j, 0, 0))],
        out_specs=pl.BlockSpec((tm, d), lambda i, j: (i, j)),
        out_shape=jax.ShapeDtypeStruct((n, N_COLS * d), BF16),
        scratch_shapes=[pltpu.VMEM((tm, d), BF16)],
        compiler_params=_params(("parallel", "arbitrary")),
        name="inproj",
    )(x2, sh, sc, w3)


def _rnn_kernel(xr_ref, yr_ref, g_ref, cw_ref, cb_ref, wa_ref, ba_ref, wi_ref, bi_ref,
                lam_ref, p_ref, o_ref, xe_scr, h_scr):
    t = pl.program_id(1)
    tt, d = xr_ref.shape
    pad = 8

    @pl.when(t == 0)
    def _():
        xe_scr[0:pad, :] = jnp.zeros((pad, d), F32)
        h_scr[...] = jnp.zeros_like(h_scr)

    x = xr_ref[...].astype(F32)
    xe_scr[pad:pad + tt, :] = x
    cw = cw_ref[...]
    xc = cb_ref[...] + cw[CONV_W - 1:CONV_W, :] * x
    for k in range(1, CONV_W):
        xc = xc + cw[CONV_W - 1 - k:CONV_W - k, :] * xe_scr[pad - k:pad - k + tt, :]
    xe_scr[0:pad, :] = x[tt - pad:tt, :]

    xcb = xc.astype(BF16)
    n_g = d // MXU_TILE
    ra = jnp.concatenate(
        [jnp.dot(xcb[:, g * MXU_TILE:(g + 1) * MXU_TILE], wa_ref[g], preferred_element_type=F32)
         for g in range(n_g)], axis=-1)
    ri = jnp.concatenate(
        [jnp.dot(xcb[:, g * MXU_TILE:(g + 1) * MXU_TILE], wi_ref[g], preferred_element_type=F32)
         for g in range(n_g)], axis=-1)
    r = jax.nn.sigmoid(ra + ba_ref[...])
    gi = jax.nn.sigmoid(ri + bi_ref[...])
    z = -lam_ref[...]
    softplus = jnp.maximum(z, 0.0) + jnp.log1p(jnp.exp(-jnp.abs(z)))
    log_a = (-LRU_C) * r * softplus
    a = jnp.exp(log_a)
    mult = jnp.sqrt(jnp.maximum(1.0 - a * a, 0.0))
    u = xc * gi * mult

    row = lax.broadcasted_iota(jnp.int32, (tt, d), 0)
    step = 1
    while step < tt:
        valid = row >= step
        a_prev = pltpu.roll(a, step, 0)
        u_prev = pltpu.roll(u, step, 0)
        u = jnp.where(valid, a * u_prev + u, u)
        a = jnp.where(valid, a * a_prev, a)
        step *= 2
    h = a * h_scr[...] + u
    h_scr[...] = h[tt - 1:tt, :]

    ya = (h * jax.nn.gelu(yr_ref[...].astype(F32))).astype(BF16)
    branch = jnp.dot(ya, p_ref[...], preferred_element_type=F32)
    o_ref[...] = (g_ref[...].astype(F32) * branch).astype(BF16)


def _rnn(proj, cw, cb, wa, ba, wi, bi, lam, p_rnn, bsz, seq, tt):
    n = proj.shape[0]
    d = D_MODEL
    per_b = seq // tt
    row = lambda b, t: b * per_b + t
    vec = pl.BlockSpec((1, d), lambda b, t: (0, 0))
    gate_w = pl.BlockSpec(wa.shape, lambda b, t: (0, 0, 0))
    return pl.pallas_call(
        _rnn_kernel,
        grid=(bsz, per_b),
        in_specs=[pl.BlockSpec((tt, d), lambda b, t: (row(b, t), COL_X)),
                  pl.BlockSpec((tt, d), lambda b, t: (row(b, t), COL_Y)),
                  pl.BlockSpec((tt, d), lambda b, t: (row(b, t), COL_GR)),
                  pl.BlockSpec((CONV_W, d), lambda b, t: (0, 0)),
                  vec, gate_w, vec, gate_w, vec, vec,
                  pl.BlockSpec((d, d), lambda b, t: (0, 0))],
        out_specs=pl.BlockSpec((tt, d), lambda b, t: (row(b, t), 0)),
        out_shape=jax.ShapeDtypeStruct((n, d), BF16),
        scratch_shapes=[pltpu.VMEM((tt + 8, d), F32), pltpu.VMEM((1, d), F32)],
        compiler_params=_params(("parallel", "arbitrary")),
        name="rnn",
    )(proj, proj, proj, cw, cb, wa, ba, wi, bi, lam, p_rnn)


def _attn_kernel(q_ref, k_ref, v_ref, lq1_ref, lk1_ref, lq2_ref, lk2_ref, sg_ref, o_ref, *, tq):
    seq = k_ref.shape[0]
    lam = (jnp.exp(jnp.sum(lq1_ref[...] * lk1_ref[...], axis=-1, keepdims=True))
           - jnp.exp(jnp.sum(lq2_ref[...] * lk2_ref[...], axis=-1, keepdims=True)) + LAM_INIT)
    nt = (((1,), (1,)), ((), ()))
    scale = DIFF_DH ** -0.5
    lane = lax.broadcasted_iota(jnp.int32, (tq, HEAD_W), 1)
    diag = (lax.broadcasted_iota(jnp.int32, (tq, tq), 1) // CHUNK
            <= lax.broadcasted_iota(jnp.int32, (tq, tq), 0) // CHUNK)

    def softmax_pv(qh, k, v):
        s = lax.dot_general(qh, k, nt, preferred_element_type=F32)
        kv = s.shape[1]
        s_diag = jnp.where(diag, s[:, kv - tq:], -jnp.inf)
        s = s_diag if kv == tq else jnp.concatenate([s[:, :kv - tq], s_diag], axis=1)
        e = jnp.exp(s - jnp.max(s, axis=-1, keepdims=True))
        pv = jnp.dot(e.astype(BF16), v, preferred_element_type=F32)
        return pv / jnp.sum(e, axis=-1, keepdims=True)

    for qi in range(seq // tq):
        kv = (qi + 1) * tq
        q = q_ref[qi * tq:(qi + 1) * tq, :] * scale
        zero = jnp.zeros_like(q)
        k = k_ref[0:kv, :]
        v = v_ref[0:kv, :]
        o = (softmax_pv(jnp.where(lane < DIFF_DH, q, zero), k, v)
             - lam * softmax_pv(jnp.where(lane >= DIFF_DH, q, zero), k, v))
        o = o * lax.rsqrt(jnp.mean(o * o, axis=-1, keepdims=True) + RMS_EPS) * sg_ref[...]
        o_ref[qi * tq:(qi + 1) * tq, :] = (o * (1.0 - LAM_INIT)).astype(BF16)


def _attn(proj, lq1, lk1, lq2, lk2, sg, bsz, seq, tq):
    n = proj.shape[0]
    hb = D_MODEL // HEAD_W
    lam_spec = pl.BlockSpec((1, DIFF_DH), lambda b, h: (0, 0))
    return pl.pallas_call(
        functools.partial(_attn_kernel, tq=tq),
        grid=(bsz, N_HEADS),
        in_specs=[pl.BlockSpec((seq, HEAD_W), lambda b, h: (b, COL_Q * hb + h)),
                  pl.BlockSpec((seq, HEAD_W), lambda b, h: (b, COL_K * hb + h)),
                  pl.BlockSpec((seq, HEAD_W), lambda b, h: (b, COL_V * hb + h)),
                  lam_spec, lam_spec, lam_spec, lam_spec,
                  pl.BlockSpec((1, HEAD_W), lambda b, h: (0, 0))],
        out_specs=pl.BlockSpec((seq, HEAD_W), lambda b, h: (b, h)),
        out_shape=jax.ShapeDtypeStruct((n, D_MODEL), BF16),
        compiler_params=_params(("parallel", "parallel")),
        name="attn",
    )(proj, proj, proj, lq1, lk1, lq2, lk2, sg)


def _postmix_kernel(yb_ref, ma_ref, ga_ref, x_ref, g1_ref, sh_ref, sc_ref, lg_ref, lb_ref,
                    pa_ref, wo_ref, wr_ref, x1_ref, u_ref, lt_ref):
    branch_b = jnp.dot(yb_ref[...], pa_ref[...], preferred_element_type=F32)
    merged = ma_ref[...].astype(F32) + ga_ref[...].astype(F32) * branch_b
    mix = jnp.dot(merged.astype(BF16), wo_ref[...], preferred_element_type=F32)
    x1 = _ln0(ALPHA * x_ref[...] + g1_ref[0] * mix) * lg_ref[...] + lb_ref[...]
    x1_ref[...] = x1
    u = _ln0(x1) * (1.0 + sc_ref[0]) + sh_ref[0]
    u_ref[...] = u.astype(BF16)
    nt = (((1,), (1,)), ((), ()))
    lt_ref[...] = lax.dot_general(wr_ref[...], u, nt, preferred_element_type=F32,
                                  precision=lax.Precision.HIGHEST)


def _postmix(yb, ma, proj, x2, g1, sh2, sc2, ln_g, ln_b, p_attn, w_out, wr_t, seq, tm):
    n, d = x2.shape
    per_b = seq // tm
    row = pl.BlockSpec((tm, d), lambda i: (i, 0))
    mod = pl.BlockSpec((1, 1, d), lambda i: (i // per_b, 0, 0))
    vec = pl.BlockSpec((1, d), lambda i: (0, 0))
    mat = pl.BlockSpec((d, d), lambda i: (0, 0))
    return pl.pallas_call(
        _postmix_kernel,
        grid=(n // tm,),
        in_specs=[row, row, pl.BlockSpec((tm, d), lambda i: (i, COL_GA)), row,
                  mod, mod, mod, vec, vec, mat, mat,
                  pl.BlockSpec((N_EXPERTS, d), lambda i: (0, 0))],
        out_specs=[row, row, pl.BlockSpec((N_EXPERTS, tm), lambda i: (0, i))],
        out_shape=[jax.ShapeDtypeStruct((n, d), F32),
                   jax.ShapeDtypeStruct((n, d), BF16),
                   jax.ShapeDtypeStruct((N_EXPERTS, n), F32)],
        compiler_params=_params(("parallel",)),
        name="postmix",
    )(yb, ma, proj, x2, g1, sh2, sc2, ln_g, ln_b, p_attn, w_out, wr_t)


def _route_kernel(lt_ref, bias_ref, gate_ref):
    tn = lt_ref.shape[1]
    shape3 = (N_GROUPS, GROUP_SIZE, tn)
    neg = -jnp.inf
    s = jax.nn.sigmoid(lt_ref[...])
    sel = (s + bias_ref[...]).reshape(shape3)
    s = s.reshape(shape3)
    j_idx = lax.broadcasted_iota(jnp.int32, shape3, 1).astype(F32)
    g_idx = lax.broadcasted_iota(jnp.int32, shape3, 0).astype(F32)
    e_idx = g_idx * GROUP_SIZE + j_idx

    m1 = jnp.max(sel, axis=1, keepdims=True)
    first = jnp.min(jnp.where(sel == m1, j_idx, float(GROUP_SIZE)), axis=1, keepdims=True)
    m2 = jnp.max(jnp.where(j_idx == first, neg, sel), axis=1, keepdims=True)
    gscore = m1 + m2

    gi = lax.broadcasted_iota(jnp.int32, gscore.shape, 0).astype(F32)
    gkeep = jnp.zeros(gscore.shape, jnp.bool_)
    for _ in range(TOPK_GROUPS):
        m = jnp.max(gscore, axis=0, keepdims=True)
        pick = jnp.min(jnp.where(gscore == m, gi, float(N_GROUPS)), axis=0, keepdims=True)
        hit = gi == pick
        gkeep = jnp.logical_or(gkeep, hit)
        gscore = jnp.where(hit, neg, gscore)
    sel = jnp.where(gkeep, sel, neg)

    picked = jnp.zeros(shape3, F32)
    for _ in range(TOP_K):
        m = jnp.max(jnp.max(sel, axis=1, keepdims=True), axis=0, keepdims=True)
        cand = jnp.where(sel == m, e_idx, float(N_EXPERTS))
        pick = jnp.min(jnp.min(cand, axis=1, keepdims=True), axis=0, keepdims=True)
        hit = e_idx == pick
        picked = jnp.where(hit, s, picked)
        sel = jnp.where(hit, neg, sel)
    total = jnp.sum(jnp.sum(picked, axis=1, keepdims=True), axis=0, keepdims=True)
    gate = picked / total * ROUTED_SCALE
    gate_ref[...] = gate.reshape(N_EXPERTS, tn)


def _route(logits_t, bias, tn):
    e, n = logits_t.shape
    return pl.pallas_call(
        _route_kernel,
        grid=(n // tn,),
        in_specs=[pl.BlockSpec((e, tn), lambda i: (0, i)),
                  pl.BlockSpec((e, 1), lambda i: (0, 0))],
        out_specs=pl.BlockSpec((e, tn), lambda i: (0, i)),
        out_shape=jax.ShapeDtypeStruct((e, n), F32),
        compiler_params=_params(("parallel",)),
        name="route",
    )(logits_t, bias.reshape(e, 1))


def _moe_kernel(u_ref, gate_ref, x1_ref, g2_ref, wg_ref, wu_ref, wd_ref, sg_ref, su_ref, sd_ref,
                lg_ref, lb_ref, o_ref, acc_scr):
    e = pl.program_id(1)
    u = u_ref[...]

    @pl.when(e == 0)
    def _():
        hs = _silu(jnp.dot(u, sg_ref[...], preferred_element_type=F32)) * jnp.dot(
            u, su_ref[...], preferred_element_type=F32)
        acc_scr[...] = jnp.dot(hs.astype(BF16), sd_ref[...], preferred_element_type=F32)

    gate = gate_ref[...]
    lane = lax.broadcasted_iota(jnp.int32, gate.shape, 1)
    gk = jnp.sum(jnp.where(lane == e, gate, 0.0), axis=-1, keepdims=True)
    hg = jnp.dot(u, wg_ref[0], preferred_element_type=F32)
    hu = jnp.dot(u, wu_ref[0], preferred_element_type=F32)
    act = (_silu(hg) * hu * gk).astype(BF16)
    acc_scr[...] += jnp.dot(act, wd_ref[0], preferred_element_type=F32)

    @pl.when(e == pl.num_programs(1) - 1)
    def _():
        z = ALPHA * x1_ref[...] + g2_ref[0] * acc_scr[...]
        o_ref[...] = _ln0(z) * lg_ref[...] + lb_ref[...]


def _moe(u, gate, x1, g2, wg, wu, wd, sg, su, sd, ln_g, ln_b, seq, tm):
    n, d = x1.shape
    n_e, _, f = wg.shape
    per_b = seq // tm
    row = pl.BlockSpec((tm, d), lambda i, e: (i, 0))
    vec = pl.BlockSpec((1, d), lambda i, e: (0, 0))
    return pl.pallas_call(
        _moe_kernel,
        grid=(n // tm, n_e),
        in_specs=[row, pl.BlockSpec((tm, n_e), lambda i, e: (i, 0)), row,
                  pl.BlockSpec((1, 1, d), lambda i, e: (i // per_b, 0, 0)),
                  pl.BlockSpec((1, d, f), lambda i, e: (e, 0, 0)),
                  pl.BlockSpec((1, d, f), lambda i, e: (e, 0, 0)),
                  pl.BlockSpec((1, f, d), lambda i, e: (e, 0, 0)),
                  pl.BlockSpec(sg.shape, lambda i, e: (0, 0)),
                  pl.BlockSpec(su.shape, lambda i, e: (0, 0)),
                  pl.BlockSpec(sd.shape, lambda i, e: (0, 0)),
                  vec, vec],
        out_specs=row,
        out_shape=jax.ShapeDtypeStruct((n, d), F32),
        scratch_shapes=[pltpu.VMEM((tm, d), F32)],
        compiler_params=_params(("parallel", "arbitrary")),
        name="moe",
    )(u, gate, x1, g2, wg, wu, wd, sg, su, sd, ln_g, ln_b)


def _block_diag_tiles(w):
    nb, bw, _ = w.shape
    per = MXU_TILE // bw
    w = w.reshape(nb // per, per, bw, bw)
    eye = jnp.eye(per, dtype=w.dtype)
    tiles = jnp.einsum("gpij,pq->gpiqj", w, eye)
    return tiles.reshape(nb // per, MXU_TILE, MXU_TILE)


def kernel(x, c, w_ada, b_ada, w_in, conv_w, conv_b, lru_w_a, lru_b_a, lru_w_i, lru_b_i, lru_lambda,
           lam_q1, lam_k1, lam_q2, lam_k2, subln_g, p_rnn, p_attn, w_out, ln1_g, ln1_b,
           w_router, router_bias, we_gate, we_up, we_down, ws_gate, ws_up, ws_down, ln2_g, ln2_b):
    bsz, seq, d = x.shape
    n = bsz * seq
    x2 = x.reshape(n, d)
    l = 0

    ada = _ada(c, w_ada[l], b_ada[l])
    sh1, sc1, g1, sh2, sc2, g2 = [ada[:, i * d:(i + 1) * d].reshape(bsz, 1, d) for i in range(6)]

    w3 = w_in[l].astype(BF16).reshape(d, N_COLS, d).transpose(1, 0, 2)
    proj = _inproj(x2, sh1, sc1, w3, seq, tm=1024)

    ma = _rnn(proj, conv_w[l], conv_b[l].reshape(1, d),
              _block_diag_tiles(lru_w_a[l]).astype(BF16), lru_b_a[l].reshape(1, d),
              _block_diag_tiles(lru_w_i[l]).astype(BF16), lru_b_i[l].reshape(1, d),
              lru_lambda[l].reshape(1, d), p_rnn[l].astype(BF16), bsz, seq, tt=256)

    yb = _attn(proj, lam_q1[l].reshape(1, -1), lam_k1[l].reshape(1, -1),
               lam_q2[l].reshape(1, -1), lam_k2[l].reshape(1, -1),
               subln_g[l].reshape(1, -1), bsz, seq, tq=256)

    x1, u, logits_t = _postmix(yb, ma, proj, x2, g1, sh2, sc2, ln1_g[l].reshape(1, d),
                               ln1_b[l].reshape(1, d), p_attn[l].astype(BF16),
                               w_out[l].astype(BF16), w_router[l].T, seq, tm=512)

    gate = _route(logits_t, router_bias[l], tn=1024).T

    out = _moe(u, gate, x1, g2, we_gate[l].astype(BF16), we_up[l].astype(BF16),
               we_down[l].astype(BF16), ws_gate[l].astype(BF16), ws_up[l].astype(BF16),
               ws_down[l].astype(BF16), ln2_g[l].reshape(1, d), ln2_b[l].reshape(1, d),
               seq, tm=1024)
    return out.reshape(bsz, seq, d)
```

```python
import functools
import math

import jax
import jax.numpy as jnp
from jax import lax
from jax.experimental import pallas as pl
from jax.experimental.pallas import tpu as pltpu

F32 = jnp.float32
BF16 = jnp.bfloat16

D_MODEL = 1024
CHUNK = 64
RNN_BLOCKS = 16
CONV_W = 4
LRU_C = 8.0
N_HEADS = 8
DIFF_DH = 64
HEAD_W = 2 * DIFF_DH
N_EXPERTS = 64
TOP_K = 8
N_GROUPS = 8
GROUP_SIZE = N_EXPERTS // N_GROUPS
TOPK_GROUPS = 4
D_EXPERT = 256
ROUTED_SCALE = 2.5
DEPTH = 1
ALPHA = (2.0 * DEPTH) ** 0.25
LN_EPS = 1e-5
RMS_EPS = 1e-5
LAM_INIT = 0.8 - 0.6 * math.exp(-0.3 * 0)

COL_X, COL_Y, COL_Q, COL_K, COL_V, COL_GR, COL_GA = range(7)
N_COLS = 7

V7X_VMEM_LIMIT = 56 * 1024 * 1024
MXU_TILE = 256
POSTMIX_CHUNK = 512


def _ln0(x):
    mu = jnp.mean(x, axis=-1, keepdims=True)
    xc = x - mu
    var = jnp.mean(xc * xc, axis=-1, keepdims=True)
    return xc * lax.rsqrt(var + LN_EPS)


def _silu(x):
    return x * jax.nn.sigmoid(x)


def _params(sem):
    return pltpu.CompilerParams(dimension_semantics=sem, vmem_limit_bytes=V7X_VMEM_LIMIT)


def _ada_kernel(c_ref, w_ref, b_ref, o_ref):
    cond = _silu(c_ref[...])
    o_ref[...] = jnp.dot(cond, w_ref[...], preferred_element_type=F32,
                         precision=lax.Precision.HIGHEST) + b_ref[...]


def _ada(c, w, b):
    bsz, d = c.shape
    n_out = w.shape[1]
    return pl.pallas_call(
        _ada_kernel,
        grid=(n_out // d,),
        in_specs=[pl.BlockSpec((bsz, d), lambda j: (0, 0)),
                  pl.BlockSpec((d, d), lambda j: (0, j)),
                  pl.BlockSpec((1, d), lambda j: (0, j))],
        out_specs=pl.BlockSpec((bsz, d), lambda j: (0, j)),
        out_shape=jax.ShapeDtypeStruct((bsz, n_out), F32),
        compiler_params=_params(("parallel",)),
        name="ada",
    )(c, w, b.reshape(1, n_out))


def _inproj_kernel(x_ref, sh_ref, sc_ref, w_ref, o_ref, h_scr):
    j = pl.program_id(1)

    @pl.when(j == 0)
    def _():
        h = _ln0(x_ref[...]) * (1.0 + sc_ref[0]) + sh_ref[0]
        h_scr[...] = h.astype(BF16)

    acc = jnp.dot(h_scr[...], w_ref[0], preferred_element_type=F32)
    o_ref[...] = jnp.where(j >= COL_GR, jax.nn.sigmoid(acc), acc).astype(BF16)


def _inproj(x2, sh, sc, w3, seq, tm):
    n, d = x2.shape
    per_b = seq // tm
    return pl.pallas_call(
        _inproj_kernel,
        grid=(n // tm, N_COLS),
        in_specs=[pl.BlockSpec((tm, d), lambda i, j: (i, 0)),
                  pl.BlockSpec((1, 1, d), lambda i, j: (i // per_b, 0, 0)),
                  pl.BlockSpec((1, 1, d), lambda i, j: (i // per_b, 0, 0)),
                  pl.BlockSpec((1, d, d), lambda i, j: (j, 0, 0))],
        out_specs=pl.BlockSpec((tm, d), lambda i, j: (i, j)),
        out_shape=jax.ShapeDtypeStruct((n, N_COLS * d), BF16),
        scratch_shapes=[pltpu.VMEM((tm, d), BF16)],
        compiler_params=_params(("parallel", "arbitrary")),
        name="inproj",
    )(x2, sh, sc, w3)


def _rnn_kernel(xr_ref, yr_ref, g_ref, cw_ref, cb_ref, wa_ref, ba_ref, wi_ref, bi_ref,
                lam_ref, p_ref, o_ref, xe_scr, h_scr):
    t = pl.program_id(1)
    tt, d = xr_ref.shape
    pad = 8

    @pl.when(t == 0)
    def _():
        xe_scr[0:pad, :] = jnp.zeros((pad, d), F32)
        h_scr[...] = jnp.zeros_like(h_scr)

    x = xr_ref[...].astype(F32)
    xe_scr[pad:pad + tt, :] = x
    cw = cw_ref[...]
    xc = cb_ref[...] + cw[CONV_W - 1:CONV_W, :] * x
    for k in range(1, CONV_W):
        xc = xc + cw[CONV_W - 1 - k:CONV_W - k, :] * xe_scr[pad - k:pad - k + tt, :]
    xe_scr[0:pad, :] = x[tt - pad:tt, :]

    xcb = xc.astype(BF16)
    n_g = d // MXU_TILE
    ra = jnp.concatenate(
        [jnp.dot(xcb[:, g * MXU_TILE:(g + 1) * MXU_TILE], wa_ref[g], preferred_element_type=F32)
         for g in range(n_g)], axis=-1)
    ri = jnp.concatenate(
        [jnp.dot(xcb[:, g * MXU_TILE:(g + 1) * MXU_TILE], wi_ref[g], preferred_element_type=F32)
         for g in range(n_g)], axis=-1)
    r = jax.nn.sigmoid(ra + ba_ref[...])
    gi = jax.nn.sigmoid(ri + bi_ref[...])
    z = -lam_ref[...]
    softplus = jnp.maximum(z, 0.0) + jnp.log1p(jnp.exp(-jnp.abs(z)))
    log_a = (-LRU_C) * r * softplus
    a = jnp.exp(log_a)
    mult = jnp.sqrt(jnp.maximum(1.0 - a * a, 0.0))
    u = xc * gi * mult

    row = lax.broadcasted_iota(jnp.int32, (tt, d), 0)
    step = 1
    while step < tt:
        valid = row >= step
        a_prev = pltpu.roll(a, step, 0)
        u_prev = pltpu.roll(u, step, 0)
        u = jnp.where(valid, a * u_prev + u, u)
        a = jnp.where(valid, a * a_prev, a)
        step *= 2
    h = a * h_scr[...] + u
    h_scr[...] = h[tt - 1:tt, :]

    ya = (h * jax.nn.gelu(yr_ref[...].astype(F32))).astype(BF16)
    branch = jnp.dot(ya, p_ref[...], preferred_element_type=F32)
    o_ref[...] = (g_ref[...].astype(F32) * branch).astype(BF16)


def _rnn(proj, cw, cb, wa, ba, wi, bi, lam, p_rnn, bsz, seq, tt):
    n = proj.shape[0]
    d = D_MODEL
    per_b = seq // tt
    row = lambda b, t: b * per_b + t
    vec = pl.BlockSpec((1, d), lambda b, t: (0, 0))
    gate_w = pl.BlockSpec(wa.shape, lambda b, t: (0, 0, 0))
    return pl.pallas_call(
        _rnn_kernel,
        grid=(bsz, per_b),
        in_specs=[pl.BlockSpec((tt, d), lambda b, t: (row(b, t), COL_X)),
                  pl.BlockSpec((tt, d), lambda b, t: (row(b, t), COL_Y)),
                  pl.BlockSpec((tt, d), lambda b, t: (row(b, t), COL_GR)),
                  pl.BlockSpec((CONV_W, d), lambda b, t: (0, 0)),
                  vec, gate_w, vec, gate_w, vec, vec,
                  pl.BlockSpec((d, d), lambda b, t: (0, 0))],
        out_specs=pl.BlockSpec((tt, d), lambda b, t: (row(b, t), 0)),
        out_shape=jax.ShapeDtypeStruct((n, d), BF16),
        scratch_shapes=[pltpu.VMEM((tt + 8, d), F32), pltpu.VMEM((1, d), F32)],
        compiler_params=_params(("parallel", "arbitrary")),
        name="rnn",
    )(proj, proj, proj, cw, cb, wa, ba, wi, bi, lam, p_rnn)


def _attn_kernel(q_ref, k_ref, v_ref, lq1_ref, lk1_ref, lq2_ref, lk2_ref, sg_ref, o_ref, *, tq):
    seq = k_ref.shape[0]
    lam = (jnp.exp(jnp.sum(lq1_ref[...] * lk1_ref[...], axis=-1, keepdims=True))
           - jnp.exp(jnp.sum(lq2_ref[...] * lk2_ref[...], axis=-1, keepdims=True)) + LAM_INIT)
    nt = (((1,), (1,)), ((), ()))
    scale = DIFF_DH ** -0.5
    lane = lax.broadcasted_iota(jnp.int32, (tq, HEAD_W), 1)
    diag = (lax.broadcasted_iota(jnp.int32, (tq, tq), 1) // CHUNK
            <= lax.broadcasted_iota(jnp.int32, (tq, tq), 0) // CHUNK)

    diag2 = jnp.concatenate([diag, diag], axis=0)

    for qi in range(seq // tq):
        kv = (qi + 1) * tq
        q = q_ref[qi * tq:(qi + 1) * tq, :] * scale
        zero = jnp.zeros_like(q)
        qq = jnp.concatenate([jnp.where(lane < DIFF_DH, q, zero), jnp.where(lane >= DIFF_DH, q, zero)], axis=0)
        s = lax.dot_general(qq, k_ref[0:kv, :], nt, preferred_element_type=F32)
        s_diag = jnp.where(diag2, s[:, kv - tq:], -jnp.inf)
        s = s_diag if kv == tq else jnp.concatenate([s[:, :kv - tq], s_diag], axis=1)
        e = jnp.exp(s - jnp.max(s, axis=-1, keepdims=True))
        pv = jnp.dot(e.astype(BF16), v_ref[0:kv, :], preferred_element_type=F32)
        pv = pv / jnp.sum(e, axis=-1, keepdims=True)
        o = pv[:tq] - lam * pv[tq:]
        o = o * lax.rsqrt(jnp.mean(o * o, axis=-1, keepdims=True) + RMS_EPS) * sg_ref[...]
        o_ref[qi * tq:(qi + 1) * tq, :] = (o * (1.0 - LAM_INIT)).astype(BF16)


def _attn(proj, lq1, lk1, lq2, lk2, sg, bsz, seq, tq):
    n = proj.shape[0]
    hb = D_MODEL // HEAD_W
    lam_spec = pl.BlockSpec((1, DIFF_DH), lambda b, h: (0, 0))
    return pl.pallas_call(
        functools.partial(_attn_kernel, tq=tq),
        grid=(bsz, N_HEADS),
        in_specs=[pl.BlockSpec((seq, HEAD_W), lambda b, h: (b, COL_Q * hb + h)),
                  pl.BlockSpec((seq, HEAD_W), lambda b, h: (b, COL_K * hb + h)),
                  pl.BlockSpec((seq, HEAD_W), lambda b, h: (b, COL_V * hb + h)),
                  lam_spec, lam_spec, lam_spec, lam_spec,
                  pl.BlockSpec((1, HEAD_W), lambda b, h: (0, 0))],
        out_specs=pl.BlockSpec((seq, HEAD_W), lambda b, h: (b, h)),
        out_shape=jax.ShapeDtypeStruct((n, D_MODEL), BF16),
        compiler_params=_params(("parallel", "parallel")),
        name="attn",
    )(proj, proj, proj, lq1, lk1, lq2, lk2, sg)


def _postmix_kernel(yb_ref, ma_ref, ga_ref, x_ref, g1_ref, sh_ref, sc_ref, lg_ref, lb_ref,
                    pa_ref, wo_ref, wr_ref, x1_ref, u_ref, lt_ref):
    nt = (((1,), (1,)), ((), ()))
    tm = x_ref.shape[0]
    for r0 in range(0, tm, POSTMIX_CHUNK):
        rows = slice(r0, r0 + POSTMIX_CHUNK)
        branch_b = jnp.dot(yb_ref[rows, :], pa_ref[...], preferred_element_type=F32)
        merged = ma_ref[rows, :].astype(F32) + ga_ref[rows, :].astype(F32) * branch_b
        mix = jnp.dot(merged.astype(BF16), wo_ref[...], preferred_element_type=F32)
        x1 = _ln0(ALPHA * x_ref[rows, :] + g1_ref[0] * mix) * lg_ref[...] + lb_ref[...]
        x1_ref[rows, :] = x1
        u = _ln0(x1) * (1.0 + sc_ref[0]) + sh_ref[0]
        u_ref[rows, :] = u.astype(BF16)
        lt_ref[:, rows] = lax.dot_general(wr_ref[...], u, nt, preferred_element_type=F32,
                                          precision=lax.Precision.HIGHEST)


def _postmix(yb, ma, proj, x2, g1, sh2, sc2, ln_g, ln_b, p_attn, w_out, wr_t, seq, tm):
    n, d = x2.shape
    per_b = seq // tm
    row = pl.BlockSpec((tm, d), lambda i: (i, 0))
    mod = pl.BlockSpec((1, 1, d), lambda i: (i // per_b, 0, 0))
    vec = pl.BlockSpec((1, d), lambda i: (0, 0))
    mat = pl.BlockSpec((d, d), lambda i: (0, 0))
    return pl.pallas_call(
        _postmix_kernel,
        grid=(n // tm,),
        in_specs=[row, row, pl.BlockSpec((tm, d), lambda i: (i, COL_GA)), row,
                  mod, mod, mod, vec, vec, mat, mat,
                  pl.BlockSpec((N_EXPERTS, d), lambda i: (0, 0))],
        out_specs=[row, row, pl.BlockSpec((N_EXPERTS, tm), lambda i: (0, i))],
        out_shape=[jax.ShapeDtypeStruct((n, d), F32),
                   jax.ShapeDtypeStruct((n, d), BF16),
                   jax.ShapeDtypeStruct((N_EXPERTS, n), F32)],
        compiler_params=_params(("parallel",)),
        name="postmix",
    )(yb, ma, proj, x2, g1, sh2, sc2, ln_g, ln_b, p_attn, w_out, wr_t)


def _route_kernel(lt_ref, bias_ref, gate_ref):
    tn = lt_ref.shape[1]
    shape3 = (N_GROUPS, GROUP_SIZE, tn)
    neg = -jnp.inf
    s = jax.nn.sigmoid(lt_ref[...])
    sel = (s + bias_ref[...]).reshape(shape3)
    s = s.reshape(shape3)
    j_idx = lax.broadcasted_iota(jnp.int32, shape3, 1).astype(F32)
    g_idx = lax.broadcasted_iota(jnp.int32, shape3, 0).astype(F32)
    e_idx = g_idx * GROUP_SIZE + j_idx

    m1 = jnp.max(sel, axis=1, keepdims=True)
    first = jnp.min(jnp.where(sel == m1, j_idx, float(GROUP_SIZE)), axis=1, keepdims=True)
    m2 = jnp.max(jnp.where(j_idx == first, neg, sel), axis=1, keepdims=True)
    gscore = m1 + m2

    gi = lax.broadcasted_iota(jnp.int32, gscore.shape, 0).astype(F32)
    gkeep = jnp.zeros(gscore.shape, jnp.bool_)
    for _ in range(TOPK_GROUPS):
        m = jnp.max(gscore, axis=0, keepdims=True)
        pick = jnp.min(jnp.where(gscore == m, gi, float(N_GROUPS)), axis=0, keepdims=True)
        hit = gi == pick
        gkeep = jnp.logical_or(gkeep, hit)
        gscore = jnp.where(hit, neg, gscore)
    sel = jnp.where(gkeep, sel, neg)

    picked = jnp.zeros(shape3, F32)
    for _ in range(TOP_K):
        m = jnp.max(jnp.max(sel, axis=1, keepdims=True), axis=0, keepdims=True)
        cand = jnp.where(sel == m, e_idx, float(N_EXPERTS))
        pick = jnp.min(jnp.min(cand, axis=1, keepdims=True), axis=0, keepdims=True)
        hit = e_idx == pick
        picked = jnp.where(hit, s, picked)
        sel = jnp.where(hit, neg, sel)
    total = jnp.sum(jnp.sum(picked, axis=1, keepdims=True), axis=0, keepdims=True)
    gate = picked / total * ROUTED_SCALE
    gate_ref[...] = gate.reshape(N_EXPERTS, tn)


def _route(logits_t, bias, tn):
    e, n = logits_t.shape
    return pl.pallas_call(
        _route_kernel,
        grid=(n // tn,),
        in_specs=[pl.BlockSpec((e, tn), lambda i: (0, i)),
                  pl.BlockSpec((e, 1), lambda i: (0, 0))],
        out_specs=pl.BlockSpec((e, tn), lambda i: (0, i)),
        out_shape=jax.ShapeDtypeStruct((e, n), F32),
        compiler_params=_params(("parallel",)),
        name="route",
    )(logits_t, bias.reshape(e, 1))


def _moe_kernel(u_ref, gate_ref, x1_ref, g2_ref, wg_ref, wu_ref, wd_ref, sg_ref, su_ref, sd_ref,
                lg_ref, lb_ref, o_ref, acc_scr, *, eg):
    j = pl.program_id(1)
    u = u_ref[...]

    @pl.when(j == 0)
    def _():
        hs = _silu(jnp.dot(u, sg_ref[...], preferred_element_type=F32)) * jnp.dot(
            u, su_ref[...], preferred_element_type=F32)
        acc_scr[...] = jnp.dot(hs.astype(BF16), sd_ref[...], preferred_element_type=F32)

    gate = gate_ref[...]
    lane = lax.broadcasted_iota(jnp.int32, gate.shape, 1)
    hg = jnp.dot(u, wg_ref[0], preferred_element_type=F32)
    hu = jnp.dot(u, wu_ref[0], preferred_element_type=F32)
    f = hg.shape[1] // eg
    acts = []
    for el in range(eg):
        gk = jnp.sum(jnp.where(lane == j * eg + el, gate, 0.0), axis=-1, keepdims=True)
        sl = slice(el * f, (el + 1) * f)
        acts.append((_silu(hg[:, sl]) * hu[:, sl] * gk).astype(BF16))
    act = jnp.concatenate(acts, axis=-1)
    acc_scr[...] += jnp.dot(act, wd_ref[0], preferred_element_type=F32)

    @pl.when(j == pl.num_programs(1) - 1)
    def _():
        z = ALPHA * x1_ref[...] + g2_ref[0] * acc_scr[...]
        o_ref[...] = _ln0(z) * lg_ref[...] + lb_ref[...]


def _moe(u, gate, x1, g2, wg, wu, wd, sg, su, sd, ln_g, ln_b, seq, tm, eg):
    n, d = x1.shape
    n_e, _, f = wg.shape
    per_b = seq // tm
    cat = lambda w: w.reshape(n_e // eg, eg, d, f).transpose(0, 2, 1, 3).reshape(n_e // eg, d, eg * f)
    wg, wu, wd = cat(wg), cat(wu), wd.reshape(n_e // eg, eg * f, d)
    row = pl.BlockSpec((tm, d), lambda i, j: (i, 0))
    vec = pl.BlockSpec((1, d), lambda i, j: (0, 0))
    return pl.pallas_call(
        functools.partial(_moe_kernel, eg=eg),
        grid=(n // tm, n_e // eg),
        in_specs=[row, pl.BlockSpec((tm, n_e), lambda i, j: (i, 0)), row,
                  pl.BlockSpec((1, 1, d), lambda i, j: (i // per_b, 0, 0)),
                  pl.BlockSpec((1, d, eg * f), lambda i, j: (j, 0, 0)),
                  pl.BlockSpec((1, d, eg * f), lambda i, j: (j, 0, 0)),
                  pl.BlockSpec((1, eg * f, d), lambda i, j: (j, 0, 0)),
                  pl.BlockSpec(sg.shape, lambda i, j: (0, 0)),
                  pl.BlockSpec(su.shape, lambda i, j: (0, 0)),
                  pl.BlockSpec(sd.shape, lambda i, j: (0, 0)),
                  vec, vec],
        out_specs=row,
        out_shape=jax.ShapeDtypeStruct((n, d), F32),
        scratch_shapes=[pltpu.VMEM((tm, d), F32)],
        compiler_params=_params(("parallel", "arbitrary")),
        name="moe",
    )(u, gate, x1, g2, wg, wu, wd, sg, su, sd, ln_g, ln_b)


def _block_diag_tiles(w):
    nb, bw, _ = w.shape
    per = MXU_TILE // bw
    w = w.reshape(nb // per, per, bw, bw)
    eye = jnp.eye(per, dtype=w.dtype)
    tiles = jnp.einsum("gpij,pq->gpiqj", w, eye)
    return tiles.reshape(nb // per, MXU_TILE, MXU_TILE)


def kernel(x, c, w_ada, b_ada, w_in, conv_w, conv_b, lru_w_a, lru_b_a, lru_w_i, lru_b_i, lru_lambda,
           lam_q1, lam_k1, lam_q2, lam_k2, subln_g, p_rnn, p_attn, w_out, ln1_g, ln1_b,
           w_router, router_bias, we_gate, we_up, we_down, ws_gate, ws_up, ws_down, ln2_g, ln2_b):
    bsz, seq, d = x.shape
    n = bsz * seq
    x2 = x.reshape(n, d)
    l = 0

    ada = _ada(c, w_ada[l], b_ada[l])
    sh1, sc1, g1, sh2, sc2, g2 = [ada[:, i * d:(i + 1) * d].reshape(bsz, 1, d) for i in range(6)]

    w3 = w_in[l].astype(BF16).reshape(d, N_COLS, d).transpose(1, 0, 2)
    proj = _inproj(x2, sh1, sc1, w3, seq, tm=1024)

    ma = _rnn(proj, conv_w[l], conv_b[l].reshape(1, d),
              _block_diag_tiles(lru_w_a[l]).astype(BF16), lru_b_a[l].reshape(1, d),
              _block_diag_tiles(lru_w_i[l]).astype(BF16), lru_b_i[l].reshape(1, d),
              lru_lambda[l].reshape(1, d), p_rnn[l].astype(BF16), bsz, seq, tt=256)

    yb = _attn(proj, lam_q1[l].reshape(1, -1), lam_k1[l].reshape(1, -1),
               lam_q2[l].reshape(1, -1), lam_k2[l].reshape(1, -1),
               subln_g[l].reshape(1, -1), bsz, seq, tq=256)

    x1, u, logits_t = _postmix(yb, ma, proj, x2, g1, sh2, sc2, ln1_g[l].reshape(1, d),
                               ln1_b[l].reshape(1, d), p_attn[l].astype(BF16),
                               w_out[l].astype(BF16), w_router[l].T, seq, tm=512)

    gate = _route(logits_t, router_bias[l], tn=1024).T

    out = _moe(u, gate, x1, g2, we_gate[l].astype(BF16), we_up[l].astype(BF16),
               we_down[l].astype(BF16), ws_gate[l].astype(BF16), ws_up[l].astype(BF16),
               ws_down[l].astype(BF16), ln2_g[l].reshape(1, d), ln2_b[l].reshape(1, d),
               seq, tm=1024, eg=4)
    return out.reshape(bsz, seq, d)
```

```python
import functools
import math

import jax
import jax.numpy as jnp
from jax import lax
from jax.experimental import pallas as pl
from jax.experimental.pallas import tpu as pltpu

F32 = jnp.float32
BF16 = jnp.bfloat16

D_MODEL = 1024
CHUNK = 64
RNN_BLOCKS = 16
CONV_W = 4
LRU_C = 8.0
N_HEADS = 8
DIFF_DH = 64
HEAD_W = 2 * DIFF_DH
N_EXPERTS = 64
TOP_K = 8
N_GROUPS = 8
GROUP_SIZE = N_EXPERTS // N_GROUPS
TOPK_GROUPS = 4
D_EXPERT = 256
ROUTED_SCALE = 2.5
DEPTH = 1
ALPHA = (2.0 * DEPTH) ** 0.25
LN_EPS = 1e-5
RMS_EPS = 1e-5
LAM_INIT = 0.8 - 0.6 * math.exp(-0.3 * 0)

COL_X, COL_Y, COL_Q, COL_K, COL_V, COL_GR, COL_GA = range(7)
N_COLS = 7

V7X_VMEM_LIMIT = 56 * 1024 * 1024
MXU_TILE = 256
SUBLANES = 8
LANES = 128


def _ln0(x):
    mu = jnp.mean(x, axis=-1, keepdims=True)
    xc = x - mu
    var = jnp.mean(xc * xc, axis=-1, keepdims=True)
    return xc * lax.rsqrt(var + LN_EPS)


def _silu(x):
    return x * jax.nn.sigmoid(x)


def _params(sem):
    return pltpu.CompilerParams(dimension_semantics=sem, vmem_limit_bytes=V7X_VMEM_LIMIT)


def _ada_kernel(c_ref, w_ref, b_ref, o_ref):
    cond = _silu(c_ref[...])
    o_ref[...] = jnp.dot(cond, w_ref[...], preferred_element_type=F32,
                         precision=lax.Precision.HIGHEST) + b_ref[...]


def _ada(c, w, b):
    bsz, d = c.shape
    n_out = w.shape[1]
    return pl.pallas_call(
        _ada_kernel,
        grid=(n_out // d,),
        in_specs=[pl.BlockSpec((bsz, d), lambda j: (0, 0)),
                  pl.BlockSpec((d, d), lambda j: (0, j)),
                  pl.BlockSpec((1, d), lambda j: (0, j))],
        out_specs=pl.BlockSpec((bsz, d), lambda j: (0, j)),
        out_shape=jax.ShapeDtypeStruct((bsz, n_out), F32),
        compiler_params=_params(("parallel",)),
        name="ada",
    )(c, w, b.reshape(1, n_out))


def _inproj_kernel(x_ref, sh_ref, sc_ref, w_ref, o_ref, h_scr):
    j = pl.program_id(1)

    @pl.when(j == 0)
    def _():
        h = _ln0(x_ref[...]) * (1.0 + sc_ref[0]) + sh_ref[0]
        h_scr[...] = h.astype(BF16)

    acc = jnp.dot(h_scr[...], w_ref[...], preferred_element_type=F32)
    o_ref[...] = jnp.where(j >= COL_GR, jax.nn.sigmoid(acc), acc).astype(BF16)


def _inproj(x2, sh, sc, w, seq, tm):
    n, d = x2.shape
    per_b = seq // tm
    return pl.pallas_call(
        _inproj_kernel,
        grid=(n // tm, N_COLS),
        in_specs=[pl.BlockSpec((tm, d), lambda i, j: (i, 0)),
                  pl.BlockSpec((1, 1, d), lambda i, j: (i // per_b, 0, 0)),
                  pl.BlockSpec((1, 1, d), lambda i, j: (i // per_b, 0, 0)),
                  pl.BlockSpec((d, d), lambda i, j: (0, j))],
        out_specs=pl.BlockSpec((tm, d), lambda i, j: (i, j)),
        out_shape=jax.ShapeDtypeStruct((n, N_COLS * d), BF16),
        scratch_shapes=[pltpu.VMEM((tm, d), BF16)],
        compiler_params=_params(("parallel", "arbitrary")),
        name="inproj",
    )(x2, sh, sc, w)


def _stream_pitch(n_slab):
    assert n_slab % SUBLANES == 0
    return n_slab if (n_slab // SUBLANES) % 2 else n_slab + SUBLANES


def _rnn_kernel(xr_ref, yr_ref, g_ref, cw_ref, cb_ref, wa_ref, ba_ref, wi_ref, bi_ref,
                lam_ref, p_ref, o_ref, x_scr, y_scr, g_scr, o_scr, tail_scr, h_scr):
    t = pl.program_id(1)
    tt, d = xr_ref.shape
    n_slab = tt // SUBLANES
    n_hist = CONV_W - 1

    @pl.when(t == 0)
    def _():
        tail_scr[...] = jnp.zeros_like(tail_scr)
        h_scr[...] = jnp.zeros_like(h_scr)

    def slab(v, j):
        return v[j * SUBLANES:(j + 1) * SUBLANES, :]

    n_lane = d // LANES
    pitch = _stream_pitch(n_slab)

    def permuted(src_ref, scr):
        x = src_ref[...].astype(F32)
        for c in range(n_lane):
            for s in range(SUBLANES):
                scr[c, s * pitch:s * pitch + n_slab, :] = x[s * n_slab:(s + 1) * n_slab, c * LANES:(c + 1) * LANES]
        return [jnp.concatenate([scr[c, pl.ds(j, SUBLANES, stride=pitch), :] for c in range(n_lane)],
                                axis=-1) for j in range(n_slab)]

    sub = lax.broadcasted_iota(jnp.int32, (SUBLANES, d), 0)

    xs = permuted(xr_ref, x_scr)
    wrapped = []
    for i in range(n_hist):
        cur = xs[n_slab - n_hist + i]
        prev = slab(tail_scr, i)
        wrapped.append(pltpu.roll(jnp.where(sub == SUBLANES - 1, prev, cur), 1, 0))
        tail_scr[i * SUBLANES:(i + 1) * SUBLANES, :] = cur

    def x_at(j):
        return xs[j] if j >= 0 else wrapped[j + n_hist]

    cw = cw_ref[...]
    w = [cw[i:i + 1, :] for i in range(CONV_W)]
    cb = cb_ref[...]
    xc = jnp.concatenate(
        [cb + w[3] * xs[j] + w[2] * x_at(j - 1) + w[1] * x_at(j - 2) + w[0] * x_at(j - 3)
         for j in range(n_slab)], axis=0)

    xcb = xc.astype(BF16)
    n_g = d // MXU_TILE
    ra = jnp.concatenate(
        [jnp.dot(xcb[:, g * MXU_TILE:(g + 1) * MXU_TILE], wa_ref[g], preferred_element_type=F32)
         for g in range(n_g)], axis=-1)
    ri = jnp.concatenate(
        [jnp.dot(xcb[:, g * MXU_TILE:(g + 1) * MXU_TILE], wi_ref[g], preferred_element_type=F32)
         for g in range(n_g)], axis=-1)
    r = jax.nn.sigmoid(ra + ba_ref[...])
    gi = jax.nn.sigmoid(ri + bi_ref[...])
    z = -lam_ref[...]
    softplus = jnp.maximum(z, 0.0) + jnp.log1p(jnp.exp(-jnp.abs(z)))
    log_a = (-LRU_C) * r * softplus
    a = jnp.exp(log_a)
    mult = jnp.sqrt(jnp.maximum(1.0 - a * a, 0.0))
    u = xc * gi * mult

    h_loc, p_loc = [slab(u, 0)], [slab(a, 0)]
    for j in range(1, n_slab):
        aj = slab(a, j)
        h_loc.append(aj * h_loc[-1] + slab(u, j))
        p_loc.append(aj * p_loc[-1])
    pa, hb = p_loc[-1], h_loc[-1]
    step = 1
    while step < SUBLANES:
        valid = sub >= step
        hb = jnp.where(valid, pa * pltpu.roll(hb, step, 0) + hb, hb)
        pa = jnp.where(valid, pa * pltpu.roll(pa, step, 0), pa)
        step *= 2
    carry = h_scr[...]
    end = pa * carry + hb
    start = jnp.where(sub == 0, carry, pltpu.roll(end, 1, 0))
    h_scr[...] = end[SUBLANES - 1:SUBLANES, :]
    h = jnp.concatenate([h_loc[j] + p_loc[j] * start for j in range(n_slab)], axis=0)

    ys = jnp.concatenate(permuted(yr_ref, y_scr), axis=0)
    gs = jnp.concatenate(permuted(g_ref, g_scr), axis=0)
    ya = (h * jax.nn.gelu(ys)).astype(BF16)
    out = gs * jnp.dot(ya, p_ref[...], preferred_element_type=F32)
    for j in range(n_slab):
        for c in range(n_lane):
            o_scr[c, pl.ds(j, SUBLANES, stride=pitch), :] = slab(out, j)[:, c * LANES:(c + 1) * LANES]
    o_ref[...] = jnp.concatenate(
        [jnp.concatenate([o_scr[c, s * pitch:s * pitch + n_slab, :] for s in range(SUBLANES)], axis=0)
         for c in range(n_lane)], axis=-1).astype(BF16)


def _rnn(proj, cw, cb, wa, ba, wi, bi, lam, p_rnn, bsz, seq, tt):
    n = proj.shape[0]
    d = D_MODEL
    per_b = seq // tt
    row = lambda b, t: b * per_b + t
    vec = pl.BlockSpec((1, d), lambda b, t: (0, 0))
    gate_w = pl.BlockSpec(wa.shape, lambda b, t: (0, 0, 0))
    tile = pltpu.VMEM((d // LANES, SUBLANES * _stream_pitch(tt // SUBLANES), LANES), F32)
    return pl.pallas_call(
        _rnn_kernel,
        grid=(bsz, per_b),
        in_specs=[pl.BlockSpec((tt, d), lambda b, t: (row(b, t), COL_X)),
                  pl.BlockSpec((tt, d), lambda b, t: (row(b, t), COL_Y)),
                  pl.BlockSpec((tt, d), lambda b, t: (row(b, t), COL_GR)),
                  pl.BlockSpec((CONV_W, d), lambda b, t: (0, 0)),
                  vec, gate_w, vec, gate_w, vec, vec,
                  pl.BlockSpec((d, d), lambda b, t: (0, 0))],
        out_specs=pl.BlockSpec((tt, d), lambda b, t: (row(b, t), 0)),
        out_shape=jax.ShapeDtypeStruct((n, d), BF16),
        scratch_shapes=[tile, tile, tile, tile,
                        pltpu.VMEM(((CONV_W - 1) * SUBLANES, d), F32), pltpu.VMEM((1, d), F32)],
        compiler_params=_params(("parallel", "arbitrary")),
        name="rnn",
    )(proj, proj, proj, cw, cb, wa, ba, wi, bi, lam, p_rnn)


def _attn_kernel(q_ref, k_ref, v_ref, lq1_ref, lk1_ref, lq2_ref, lk2_ref, sg_ref, o_ref, *, tq):
    seq = k_ref.shape[0]
    lam = (jnp.exp(jnp.sum(lq1_ref[...] * lk1_ref[...], axis=-1, keepdims=True))
           - jnp.exp(jnp.sum(lq2_ref[...] * lk2_ref[...], axis=-1, keepdims=True)) + LAM_INIT)
    nt = (((1,), (1,)), ((), ()))
    scale = DIFF_DH ** -0.5
    lane = lax.broadcasted_iota(jnp.int32, (tq, HEAD_W), 1)
    diag = (lax.broadcasted_iota(jnp.int32, (tq, tq), 1) // CHUNK
            <= lax.broadcasted_iota(jnp.int32, (tq, tq), 0) // CHUNK)

    diag2 = jnp.concatenate([diag, diag], axis=0)

    for qi in range(seq // tq):
        kv = (qi + 1) * tq
        q = q_ref[qi * tq:(qi + 1) * tq, :] * scale
        zero = jnp.zeros_like(q)
        qq = jnp.concatenate([jnp.where(lane < DIFF_DH, q, zero), jnp.where(lane >= DIFF_DH, q, zero)], axis=0)
        s = lax.dot_general(qq, k_ref[0:kv, :], nt, preferred_element_type=F32)
        s_diag = jnp.where(diag2, s[:, kv - tq:], -jnp.inf)
        s = s_diag if kv == tq else jnp.concatenate([s[:, :kv - tq], s_diag], axis=1)
        e = jnp.exp(s - jnp.max(s, axis=-1, keepdims=True))
        pv = jnp.dot(e.astype(BF16), v_ref[0:kv, :], preferred_element_type=F32)
        pv = pv / jnp.sum(e, axis=-1, keepdims=True)
        o = pv[:tq] - lam * pv[tq:]
        o = o * lax.rsqrt(jnp.mean(o * o, axis=-1, keepdims=True) + RMS_EPS) * sg_ref[...]
        o_ref[qi * tq:(qi + 1) * tq, :] = (o * (1.0 - LAM_INIT)).astype(BF16)


def _attn(proj, lq1, lk1, lq2, lk2, sg, bsz, seq, tq):
    n = proj.shape[0]
    hb = D_MODEL // HEAD_W
    lam_spec = pl.BlockSpec((1, DIFF_DH), lambda b, h: (0, 0))
    return pl.pallas_call(
        functools.partial(_attn_kernel, tq=tq),
        grid=(bsz, N_HEADS),
        in_specs=[pl.BlockSpec((seq, HEAD_W), lambda b, h: (b, COL_Q * hb + h)),
                  pl.BlockSpec((seq, HEAD_W), lambda b, h: (b, COL_K * hb + h)),
                  pl.BlockSpec((seq, HEAD_W), lambda b, h: (b, COL_V * hb + h)),
                  lam_spec, lam_spec, lam_spec, lam_spec,
                  pl.BlockSpec((1, HEAD_W), lambda b, h: (0, 0))],
        out_specs=pl.BlockSpec((seq, HEAD_W), lambda b, h: (b, h)),
        out_shape=jax.ShapeDtypeStruct((n, D_MODEL), BF16),
        compiler_params=_params(("parallel", "parallel")),
        name="attn",
    )(proj, proj, proj, lq1, lk1, lq2, lk2, sg)


def _postmix_kernel(yb_ref, ma_ref, ga_ref, x_ref, g1_ref, sh_ref, sc_ref, lg_ref, lb_ref,
                    pa_ref, wo_ref, wr_ref, x1_ref, u_ref, lt_ref):
    branch_b = jnp.dot(yb_ref[...], pa_ref[...], preferred_element_type=F32)
    merged = ma_ref[...].astype(F32) + ga_ref[...].astype(F32) * branch_b
    mix = jnp.dot(merged.astype(BF16), wo_ref[...], preferred_element_type=F32)
    x1 = _ln0(ALPHA * x_ref[...] + g1_ref[0] * mix) * lg_ref[...] + lb_ref[...]
    x1_ref[...] = x1
    u = _ln0(x1) * (1.0 + sc_ref[0]) + sh_ref[0]
    u_ref[...] = u.astype(BF16)
    nt = (((1,), (1,)), ((), ()))
    lt_ref[...] = lax.dot_general(wr_ref[...], u, nt, preferred_element_type=F32,
                                  precision=lax.Precision.HIGHEST)


def _postmix(yb, ma, proj, x2, g1, sh2, sc2, ln_g, ln_b, p_attn, w_out, wr_t, seq, tm):
    n, d = x2.shape
    per_b = seq // tm
    row = pl.BlockSpec((tm, d), lambda i: (i, 0))
    mod = pl.BlockSpec((1, 1, d), lambda i: (i // per_b, 0, 0))
    vec = pl.BlockSpec((1, d), lambda i: (0, 0))
    mat = pl.BlockSpec((d, d), lambda i: (0, 0))
    return pl.pallas_call(
        _postmix_kernel,
        grid=(n // tm,),
        in_specs=[row, row, pl.BlockSpec((tm, d), lambda i: (i, COL_GA)), row,
                  mod, mod, mod, vec, vec, mat, mat,
                  pl.BlockSpec((N_EXPERTS, d), lambda i: (0, 0))],
        out_specs=[row, row, pl.BlockSpec((N_EXPERTS, tm), lambda i: (0, i))],
        out_shape=[jax.ShapeDtypeStruct((n, d), F32),
                   jax.ShapeDtypeStruct((n, d), BF16),
                   jax.ShapeDtypeStruct((N_EXPERTS, n), F32)],
        compiler_params=_params(("parallel",)),
        name="postmix",
    )(yb, ma, proj, x2, g1, sh2, sc2, ln_g, ln_b, p_attn, w_out, wr_t)


def _route_kernel(lt_ref, bias_ref, gate_ref):
    tn = lt_ref.shape[1]
    shape3 = (N_GROUPS, GROUP_SIZE, tn)
    neg = -jnp.inf
    s = jax.nn.sigmoid(lt_ref[...])
    sel = (s + bias_ref[...]).reshape(shape3)
    s = s.reshape(shape3)
    j_idx = lax.broadcasted_iota(jnp.int32, shape3, 1).astype(F32)
    g_idx = lax.broadcasted_iota(jnp.int32, shape3, 0).astype(F32)
    e_idx = g_idx * GROUP_SIZE + j_idx

    m1 = jnp.max(sel, axis=1, keepdims=True)
    first = jnp.min(jnp.where(sel == m1, j_idx, float(GROUP_SIZE)), axis=1, keepdims=True)
    m2 = jnp.max(jnp.where(j_idx == first, neg, sel), axis=1, keepdims=True)
    gscore = m1 + m2

    gi = lax.broadcasted_iota(jnp.int32, gscore.shape, 0).astype(F32)
    gkeep = jnp.zeros(gscore.shape, jnp.bool_)
    for _ in range(TOPK_GROUPS):
        m = jnp.max(gscore, axis=0, keepdims=True)
        pick = jnp.min(jnp.where(gscore == m, gi, float(N_GROUPS)), axis=0, keepdims=True)
        hit = gi == pick
        gkeep = jnp.logical_or(gkeep, hit)
        gscore = jnp.where(hit, neg, gscore)
    sel = jnp.where(gkeep, sel, neg)

    picked = jnp.zeros(shape3, F32)
    for _ in range(TOP_K):
        m = jnp.max(jnp.max(sel, axis=1, keepdims=True), axis=0, keepdims=True)
        cand = jnp.where(sel == m, e_idx, float(N_EXPERTS))
        pick = jnp.min(jnp.min(cand, axis=1, keepdims=True), axis=0, keepdims=True)
        hit = e_idx == pick
        picked = jnp.where(hit, s, picked)
        sel = jnp.where(hit, neg, sel)
    total = jnp.sum(jnp.sum(picked, axis=1, keepdims=True), axis=0, keepdims=True)
    gate = picked / total * ROUTED_SCALE
    gate_ref[...] = gate.reshape(N_EXPERTS, tn)


def _route(logits_t, bias, tn):
    e, n = logits_t.shape
    return pl.pallas_call(
        _route_kernel,
        grid=(n // tn,),
        in_specs=[pl.BlockSpec((e, tn), lambda i: (0, i)),
                  pl.BlockSpec((e, 1), lambda i: (0, 0))],
        out_specs=pl.BlockSpec((e, tn), lambda i: (0, i)),
        out_shape=jax.ShapeDtypeStruct((e, n), F32),
        compiler_params=_params(("parallel",)),
        name="route",
    )(logits_t, bias.reshape(e, 1))


def _moe_kernel(u_ref, gate_ref, x1_ref, g2_ref, wg_ref, wu_ref, wd_ref, sg_ref, su_ref, sd_ref,
                lg_ref, lb_ref, o_ref, acc_scr, *, eg):
    j = pl.program_id(1)
    u = u_ref[...]

    @pl.when(j == 0)
    def _():
        hs = _silu(jnp.dot(u, sg_ref[...], preferred_element_type=F32)) * jnp.dot(
            u, su_ref[...], preferred_element_type=F32)
        acc_scr[...] = jnp.dot(hs.astype(BF16), sd_ref[...], preferred_element_type=F32)

    gate = gate_ref[...]
    lane = lax.broadcasted_iota(jnp.int32, gate.shape, 1)
    acts = []
    for el in range(eg):
        gk = jnp.sum(jnp.where(lane == j * eg + el, gate, 0.0), axis=-1, keepdims=True)
        hg = jnp.dot(u, wg_ref[el], preferred_element_type=F32)
        hu = jnp.dot(u, wu_ref[el], preferred_element_type=F32)
        acts.append((_silu(hg) * hu * gk).astype(BF16))
    act = jnp.concatenate(acts, axis=-1)
    acc_scr[...] += jnp.dot(act, wd_ref[0], preferred_element_type=F32)

    @pl.when(j == pl.num_programs(1) - 1)
    def _():
        z = ALPHA * x1_ref[...] + g2_ref[0] * acc_scr[...]
        o_ref[...] = _ln0(z) * lg_ref[...] + lb_ref[...]


def _moe(u, gate, x1, g2, wg, wu, wd, sg, su, sd, ln_g, ln_b, seq, tm, eg):
    n, d = x1.shape
    n_e, _, f = wg.shape
    per_b = seq // tm
    wd = wd.reshape(n_e // eg, eg * f, d)
    row = pl.BlockSpec((tm, d), lambda i, j: (i, 0))
    vec = pl.BlockSpec((1, d), lambda i, j: (0, 0))
    return pl.pallas_call(
        functools.partial(_moe_kernel, eg=eg),
        grid=(n // tm, n_e // eg),
        in_specs=[row, pl.BlockSpec((tm, n_e), lambda i, j: (i, 0)), row,
                  pl.BlockSpec((1, 1, d), lambda i, j: (i // per_b, 0, 0)),
                  pl.BlockSpec((eg, d, f), lambda i, j: (j, 0, 0)),
                  pl.BlockSpec((eg, d, f), lambda i, j: (j, 0, 0)),
                  pl.BlockSpec((1, eg * f, d), lambda i, j: (j, 0, 0)),
                  pl.BlockSpec(sg.shape, lambda i, j: (0, 0)),
                  pl.BlockSpec(su.shape, lambda i, j: (0, 0)),
                  pl.BlockSpec(sd.shape, lambda i, j: (0, 0)),
                  vec, vec],
        out_specs=row,
        out_shape=jax.ShapeDtypeStruct((n, d), F32),
        scratch_shapes=[pltpu.VMEM((tm, d), F32)],
        compiler_params=_params(("parallel", "arbitrary")),
        name="moe",
    )(u, gate, x1, g2, wg, wu, wd, sg, su, sd, ln_g, ln_b)


def _block_diag_tiles(w):
    nb, bw, _ = w.shape
    per = MXU_TILE // bw
    w = w.reshape(nb // per, per, bw, bw)
    eye = jnp.eye(per, dtype=w.dtype)
    tiles = jnp.einsum("gpij,pq->gpiqj", w, eye)
    return tiles.reshape(nb // per, MXU_TILE, MXU_TILE)


def kernel(x, c, w_ada, b_ada, w_in, conv_w, conv_b, lru_w_a, lru_b_a, lru_w_i, lru_b_i, lru_lambda,
           lam_q1, lam_k1, lam_q2, lam_k2, subln_g, p_rnn, p_attn, w_out, ln1_g, ln1_b,
           w_router, router_bias, we_gate, we_up, we_down, ws_gate, ws_up, ws_down, ln2_g, ln2_b):
    bsz, seq, d = x.shape
    n = bsz * seq
    x2 = x.reshape(n, d)
    l = 0

    ada = _ada(c, w_ada[l], b_ada[l])
    sh1, sc1, g1, sh2, sc2, g2 = [ada[:, i * d:(i + 1) * d].reshape(bsz, 1, d) for i in range(6)]

    proj = _inproj(x2, sh1, sc1, w_in[l].astype(BF16), seq, tm=1024)

    ma = _rnn(proj, conv_w[l], conv_b[l].reshape(1, d),
              _block_diag_tiles(lru_w_a[l]).astype(BF16), lru_b_a[l].reshape(1, d),
              _block_diag_tiles(lru_w_i[l]).astype(BF16), lru_b_i[l].reshape(1, d),
              lru_lambda[l].reshape(1, d), p_rnn[l].astype(BF16), bsz, seq, tt=256)

    yb = _attn(proj, lam_q1[l].reshape(1, -1), lam_k1[l].reshape(1, -1),
               lam_q2[l].reshape(1, -1), lam_k2[l].reshape(1, -1),
               subln_g[l].reshape(1, -1), bsz, seq, tq=256)

    x1, u, logits_t = _postmix(yb, ma, proj, x2, g1, sh2, sc2, ln1_g[l].reshape(1, d),
                               ln1_b[l].reshape(1, d), p_attn[l].astype(BF16),
                               w_out[l].astype(BF16), w_router[l].T, seq, tm=512)

    gate = _route(logits_t, router_bias[l], tn=1024).T

    out = _moe(u, gate, x1, g2, we_gate[l].astype(BF16), we_up[l].astype(BF16),
               we_down[l].astype(BF16), ws_gate[l].astype(BF16), ws_up[l].astype(BF16),
               ws_down[l].astype(BF16), ln2_g[l].reshape(1, d), ln2_b[l].reshape(1, d),
               seq, tm=1024, eg=4)
    return out.reshape(bsz, seq, d)
```

```python
import functools
import math

import jax
import jax.numpy as jnp
from jax import lax
from jax.experimental import pallas as pl
from jax.experimental.pallas import tpu as pltpu

F32 = jnp.float32
BF16 = jnp.bfloat16

D_MODEL = 1024
CHUNK = 64
RNN_BLOCKS = 16
CONV_W = 4
LRU_C = 8.0
N_HEADS = 8
DIFF_DH = 64
HEAD_W = 2 * DIFF_DH
N_EXPERTS = 64
TOP_K = 8
N_GROUPS = 8
GROUP_SIZE = N_EXPERTS // N_GROUPS
TOPK_GROUPS = 4
D_EXPERT = 256
ROUTED_SCALE = 2.5
DEPTH = 1
ALPHA = (2.0 * DEPTH) ** 0.25
LN_EPS = 1e-5
RMS_EPS = 1e-5
LAM_INIT = 0.8 - 0.6 * math.exp(-0.3 * 0)

COL_X, COL_Y, COL_Q, COL_K, COL_V, COL_GR, COL_GA = range(7)
N_COLS = 7

V7X_VMEM_LIMIT = 56 * 1024 * 1024
MXU_TILE = 256
SUBLANES = 8
LANES = 128
POSTMIX_CHUNK = 512


def _ln0(x):
    mu = jnp.mean(x, axis=-1, keepdims=True)
    xc = x - mu
    var = jnp.mean(xc * xc, axis=-1, keepdims=True)
    return xc * lax.rsqrt(var + LN_EPS)


def _silu(x):
    return x * jax.nn.sigmoid(x)


def _params(sem):
    return pltpu.CompilerParams(dimension_semantics=sem, vmem_limit_bytes=V7X_VMEM_LIMIT)


def _ada_kernel(c_ref, w_ref, b_ref, o_ref):
    cond = _silu(c_ref[...])
    o_ref[...] = jnp.dot(cond, w_ref[...], preferred_element_type=F32,
                         precision=lax.Precision.HIGHEST) + b_ref[...]


def _ada(c, w, b):
    bsz, d = c.shape
    n_out = w.shape[1]
    return pl.pallas_call(
        _ada_kernel,
        grid=(n_out // d,),
        in_specs=[pl.BlockSpec((bsz, d), lambda j: (0, 0)),
                  pl.BlockSpec((d, d), lambda j: (0, j)),
                  pl.BlockSpec((1, d), lambda j: (0, j))],
        out_specs=pl.BlockSpec((bsz, d), lambda j: (0, j)),
        out_shape=jax.ShapeDtypeStruct((bsz, n_out), F32),
        compiler_params=_params(("parallel",)),
        name="ada",
    )(c, w, b.reshape(1, n_out))


def _inproj_kernel(x_ref, sh_ref, sc_ref, w_ref, o_ref, h_scr):
    j = pl.program_id(1)

    @pl.when(j == 0)
    def _():
        h = _ln0(x_ref[...]) * (1.0 + sc_ref[0]) + sh_ref[0]
        h_scr[...] = h.astype(BF16)

    acc = jnp.dot(h_scr[...], w_ref[...].astype(BF16), preferred_element_type=F32)
    o_ref[...] = jnp.where(j >= COL_GR, jax.nn.sigmoid(acc), acc).astype(BF16)


def _inproj(x2, sh, sc, w, seq, tm):
    n, d = x2.shape
    per_b = seq // tm
    return pl.pallas_call(
        _inproj_kernel,
        grid=(n // tm, N_COLS),
        in_specs=[pl.BlockSpec((tm, d), lambda i, j: (i, 0)),
                  pl.BlockSpec((1, 1, d), lambda i, j: (i // per_b, 0, 0)),
                  pl.BlockSpec((1, 1, d), lambda i, j: (i // per_b, 0, 0)),
                  pl.BlockSpec((d, d), lambda i, j: (0, j))],
        out_specs=pl.BlockSpec((tm, d), lambda i, j: (i, j)),
        out_shape=jax.ShapeDtypeStruct((n, N_COLS * d), BF16),
        scratch_shapes=[pltpu.VMEM((tm, d), BF16)],
        compiler_params=_params(("parallel", "arbitrary")),
        name="inproj",
    )(x2, sh, sc, w)


def _stream_pitch(n_slab):
    assert n_slab % SUBLANES == 0
    return n_slab if (n_slab // SUBLANES) % 2 else n_slab + SUBLANES


def _rnn_kernel(xr_ref, yr_ref, g_ref, cw_ref, cb_ref, wa_ref, ba_ref, wi_ref, bi_ref,
                lam_ref, p_ref, o_ref, x_scr, y_scr, g_scr, o_scr, tail_scr, h_scr):
    t = pl.program_id(1)
    tt, d = xr_ref.shape
    n_slab = tt // SUBLANES
    n_hist = CONV_W - 1

    @pl.when(t == 0)
    def _():
        tail_scr[...] = jnp.zeros_like(tail_scr)
        h_scr[...] = jnp.zeros_like(h_scr)

    def slab(v, j):
        return v[j * SUBLANES:(j + 1) * SUBLANES, :]

    n_lane = d // LANES
    pitch = _stream_pitch(n_slab)

    def permuted(src_ref, scr):
        x = src_ref[...].astype(F32)
        for c in range(n_lane):
            for s in range(SUBLANES):
                scr[c, s * pitch:s * pitch + n_slab, :] = x[s * n_slab:(s + 1) * n_slab, c * LANES:(c + 1) * LANES]
        return [jnp.concatenate([scr[c, pl.ds(j, SUBLANES, stride=pitch), :] for c in range(n_lane)],
                                axis=-1) for j in range(n_slab)]

    sub = lax.broadcasted_iota(jnp.int32, (SUBLANES, d), 0)

    xs = permuted(xr_ref, x_scr)
    wrapped = []
    for i in range(n_hist):
        cur = xs[n_slab - n_hist + i]
        prev = slab(tail_scr, i)
        wrapped.append(pltpu.roll(jnp.where(sub == SUBLANES - 1, prev, cur), 1, 0))
        tail_scr[i * SUBLANES:(i + 1) * SUBLANES, :] = cur

    def x_at(j):
        return xs[j] if j >= 0 else wrapped[j + n_hist]

    cw = cw_ref[...]
    w = [cw[i:i + 1, :] for i in range(CONV_W)]
    cb = cb_ref[...]
    xc = jnp.concatenate(
        [cb + w[3] * xs[j] + w[2] * x_at(j - 1) + w[1] * x_at(j - 2) + w[0] * x_at(j - 3)
         for j in range(n_slab)], axis=0)

    xcb = xc.astype(BF16)
    n_g = d // MXU_TILE
    ra = jnp.concatenate(
        [jnp.dot(xcb[:, g * MXU_TILE:(g + 1) * MXU_TILE], wa_ref[g], preferred_element_type=F32)
         for g in range(n_g)], axis=-1)
    ri = jnp.concatenate(
        [jnp.dot(xcb[:, g * MXU_TILE:(g + 1) * MXU_TILE], wi_ref[g], preferred_element_type=F32)
         for g in range(n_g)], axis=-1)
    r = jax.nn.sigmoid(ra + ba_ref[...])
    gi = jax.nn.sigmoid(ri + bi_ref[...])
    z = -lam_ref[...]
    softplus = jnp.maximum(z, 0.0) + jnp.log1p(jnp.exp(-jnp.abs(z)))
    log_a = (-LRU_C) * r * softplus
    a = jnp.exp(log_a)
    mult = jnp.sqrt(jnp.maximum(1.0 - a * a, 0.0))
    u = xc * gi * mult

    h_loc, p_loc = [slab(u, 0)], [slab(a, 0)]
    for j in range(1, n_slab):
        aj = slab(a, j)
        h_loc.append(aj * h_loc[-1] + slab(u, j))
        p_loc.append(aj * p_loc[-1])
    pa, hb = p_loc[-1], h_loc[-1]
    step = 1
    while step < SUBLANES:
        valid = sub >= step
        hb = jnp.where(valid, pa * pltpu.roll(hb, step, 0) + hb, hb)
        pa = jnp.where(valid, pa * pltpu.roll(pa, step, 0), pa)
        step *= 2
    carry = h_scr[...]
    end = pa * carry + hb
    start = jnp.where(sub == 0, carry, pltpu.roll(end, 1, 0))
    h_scr[...] = end[SUBLANES - 1:SUBLANES, :]
    h = jnp.concatenate([h_loc[j] + p_loc[j] * start for j in range(n_slab)], axis=0)

    ys = jnp.concatenate(permuted(yr_ref, y_scr), axis=0)
    gs = jnp.concatenate(permuted(g_ref, g_scr), axis=0)
    ya = (h * jax.nn.gelu(ys)).astype(BF16)
    out = gs * jnp.dot(ya, p_ref[...], preferred_element_type=F32)
    for j in range(n_slab):
        for c in range(n_lane):
            o_scr[c, pl.ds(j, SUBLANES, stride=pitch), :] = slab(out, j)[:, c * LANES:(c + 1) * LANES]
    o_ref[...] = jnp.concatenate(
        [jnp.concatenate([o_scr[c, s * pitch:s * pitch + n_slab, :] for s in range(SUBLANES)], axis=0)
         for c in range(n_lane)], axis=-1).astype(BF16)


def _rnn(proj, cw, cb, wa, ba, wi, bi, lam, p_rnn, bsz, seq, tt):
    n = proj.shape[0]
    d = D_MODEL
    per_b = seq // tt
    row = lambda b, t: b * per_b + t
    vec = pl.BlockSpec((1, d), lambda b, t: (0, 0))
    gate_w = pl.BlockSpec(wa.shape, lambda b, t: (0, 0, 0))
    tile = pltpu.VMEM((d // LANES, SUBLANES * _stream_pitch(tt // SUBLANES), LANES), F32)
    return pl.pallas_call(
        _rnn_kernel,
        grid=(bsz, per_b),
        in_specs=[pl.BlockSpec((tt, d), lambda b, t: (row(b, t), COL_X)),
                  pl.BlockSpec((tt, d), lambda b, t: (row(b, t), COL_Y)),
                  pl.BlockSpec((tt, d), lambda b, t: (row(b, t), COL_GR)),
                  pl.BlockSpec((CONV_W, d), lambda b, t: (0, 0)),
                  vec, gate_w, vec, gate_w, vec, vec,
                  pl.BlockSpec((d, d), lambda b, t: (0, 0))],
        out_specs=pl.BlockSpec((tt, d), lambda b, t: (row(b, t), 0)),
        out_shape=jax.ShapeDtypeStruct((n, d), BF16),
        scratch_shapes=[tile, tile, tile, tile,
                        pltpu.VMEM(((CONV_W - 1) * SUBLANES, d), F32), pltpu.VMEM((1, d), F32)],
        compiler_params=_params(("parallel", "arbitrary")),
        name="rnn",
    )(proj, proj, proj, cw, cb, wa, ba, wi, bi, lam, p_rnn)


def _attn_kernel(q_ref, k_ref, v_ref, lq1_ref, lk1_ref, lq2_ref, lk2_ref, sg_ref, o_ref, *, tq):
    seq = k_ref.shape[0]
    lam = (jnp.exp(jnp.sum(lq1_ref[...] * lk1_ref[...], axis=-1, keepdims=True))
           - jnp.exp(jnp.sum(lq2_ref[...] * lk2_ref[...], axis=-1, keepdims=True)) + LAM_INIT)
    nt = (((1,), (1,)), ((), ()))
    scale = DIFF_DH ** -0.5
    lane = lax.broadcasted_iota(jnp.int32, (tq, HEAD_W), 1)
    diag = (lax.broadcasted_iota(jnp.int32, (tq, tq), 1) // CHUNK
            <= lax.broadcasted_iota(jnp.int32, (tq, tq), 0) // CHUNK)

    diag2 = jnp.concatenate([diag, diag], axis=0)

    for qi in reversed(range(seq // tq)):
        kv = (qi + 1) * tq
        q = q_ref[qi * tq:(qi + 1) * tq, :] * scale
        zero = jnp.zeros_like(q)
        qq = jnp.concatenate([jnp.where(lane < DIFF_DH, q, zero), jnp.where(lane >= DIFF_DH, q, zero)], axis=0)
        s = lax.dot_general(qq, k_ref[0:kv, :], nt, preferred_element_type=F32)
        s_diag = jnp.where(diag2, s[:, kv - tq:], -jnp.inf)
        s = s_diag if kv == tq else jnp.concatenate([s[:, :kv - tq], s_diag], axis=1)
        e = jnp.exp(s - jnp.max(s, axis=-1, keepdims=True))
        pv = jnp.dot(e.astype(BF16), v_ref[0:kv, :], preferred_element_type=F32)
        pv = pv / jnp.sum(e, axis=-1, keepdims=True)
        o = pv[:tq] - lam * pv[tq:]
        o = o * lax.rsqrt(jnp.mean(o * o, axis=-1, keepdims=True) + RMS_EPS) * sg_ref[...]
        o_ref[qi * tq:(qi + 1) * tq, :] = (o * (1.0 - LAM_INIT)).astype(BF16)


def _attn(proj, lq1, lk1, lq2, lk2, sg, bsz, seq, tq):
    n = proj.shape[0]
    hb = D_MODEL // HEAD_W
    lam_spec = pl.BlockSpec((1, DIFF_DH), lambda b, h: (0, 0))
    return pl.pallas_call(
        functools.partial(_attn_kernel, tq=tq),
        grid=(bsz, N_HEADS),
        in_specs=[pl.BlockSpec((seq, HEAD_W), lambda b, h: (b, COL_Q * hb + h)),
                  pl.BlockSpec((seq, HEAD_W), lambda b, h: (b, COL_K * hb + h)),
                  pl.BlockSpec((seq, HEAD_W), lambda b, h: (b, COL_V * hb + h)),
                  lam_spec, lam_spec, lam_spec, lam_spec,
                  pl.BlockSpec((1, HEAD_W), lambda b, h: (0, 0))],
        out_specs=pl.BlockSpec((seq, HEAD_W), lambda b, h: (b, h)),
        out_shape=jax.ShapeDtypeStruct((n, D_MODEL), BF16),
        compiler_params=_params(("parallel", "parallel")),
        name="attn",
    )(proj, proj, proj, lq1, lk1, lq2, lk2, sg)


def _postmix_kernel(yb_ref, ma_ref, ga_ref, x_ref, g1_ref, sh_ref, sc_ref, lg_ref, lb_ref,
                    pa_ref, wo_ref, wr_ref, x1_ref, u_ref, lt_ref):
    tm = x_ref.shape[0]

    def project(rows):
        branch_b = jnp.dot(yb_ref[rows, :], pa_ref[...], preferred_element_type=F32)
        merged = ma_ref[rows, :].astype(F32) + ga_ref[rows, :].astype(F32) * branch_b
        return jnp.dot(merged.astype(BF16), wo_ref[...], preferred_element_type=F32)

    def normalize(rows, mix):
        x1 = _ln0(ALPHA * x_ref[rows, :] + g1_ref[0] * mix) * lg_ref[...] + lb_ref[...]
        x1_ref[rows, :] = x1
        u = _ln0(x1) * (1.0 + sc_ref[0]) + sh_ref[0]
        u_ref[rows, :] = u.astype(BF16)
        u_hi = u.astype(BF16)
        u_lo = (u - u_hi.astype(F32)).astype(BF16)
        a = jnp.dot(u_hi, wr_ref[...], preferred_element_type=F32)
        b = jnp.dot(u_lo, wr_ref[...], preferred_element_type=F32)
        lt_ref[rows, :] = a + pltpu.roll(a, N_EXPERTS, 1) + b

    chunks = [slice(r0, r0 + POSTMIX_CHUNK) for r0 in range(0, tm, POSTMIX_CHUNK)]
    mix = project(chunks[0])
    for prev, rows in zip(chunks[:-1], chunks[1:]):
        nxt = project(rows)
        normalize(prev, mix)
        mix = nxt
    normalize(chunks[-1], mix)


def _postmix(yb, ma, proj, x2, g1, sh2, sc2, ln_g, ln_b, p_attn, w_out, wr_hilo, seq, tm):
    n, d = x2.shape
    per_b = seq // tm
    row = pl.BlockSpec((tm, d), lambda i: (i, 0))
    mod = pl.BlockSpec((1, 1, d), lambda i: (i // per_b, 0, 0))
    vec = pl.BlockSpec((1, d), lambda i: (0, 0))
    mat = pl.BlockSpec((d, d), lambda i: (0, 0))
    return pl.pallas_call(
        _postmix_kernel,
        grid=(n // tm,),
        in_specs=[row, row, pl.BlockSpec((tm, d), lambda i: (i, COL_GA)), row,
                  mod, mod, mod, vec, vec, mat, mat,
                  pl.BlockSpec((d, 2 * N_EXPERTS), lambda i: (0, 0))],
        out_specs=[row, row, pl.BlockSpec((tm, 2 * N_EXPERTS), lambda i: (i, 0))],
        out_shape=[jax.ShapeDtypeStruct((n, d), F32),
                   jax.ShapeDtypeStruct((n, d), BF16),
                   jax.ShapeDtypeStruct((n, 2 * N_EXPERTS), F32)],
        compiler_params=_params(("parallel",)),
        name="postmix",
    )(yb, ma, proj, x2, g1, sh2, sc2, ln_g, ln_b, p_attn, w_out, wr_hilo)


def _route_kernel(lt_ref, bias_ref, gate_ref):
    tn = lt_ref.shape[0]
    shape3 = (N_GROUPS, GROUP_SIZE, tn)
    neg = -jnp.inf
    s = jax.nn.sigmoid(lt_ref[...].T[:N_EXPERTS, :])
    sel = (s + bias_ref[...]).reshape(shape3)
    s = s.reshape(shape3)
    j_idx = lax.broadcasted_iota(jnp.int32, shape3, 1).astype(F32)
    g_idx = lax.broadcasted_iota(jnp.int32, shape3, 0).astype(F32)
    e_idx = g_idx * GROUP_SIZE + j_idx

    m1 = jnp.max(sel, axis=1, keepdims=True)
    first = jnp.min(jnp.where(sel == m1, j_idx, float(GROUP_SIZE)), axis=1, keepdims=True)
    m2 = jnp.max(jnp.where(j_idx == first, neg, sel), axis=1, keepdims=True)
    gscore = m1 + m2

    gi = lax.broadcasted_iota(jnp.int32, gscore.shape, 0).astype(F32)
    gkeep = jnp.zeros(gscore.shape, jnp.bool_)
    for _ in range(TOPK_GROUPS):
        m = jnp.max(gscore, axis=0, keepdims=True)
        pick = jnp.min(jnp.where(gscore == m, gi, float(N_GROUPS)), axis=0, keepdims=True)
        hit = gi == pick
        gkeep = jnp.logical_or(gkeep, hit)
        gscore = jnp.where(hit, neg, gscore)
    sel = jnp.where(gkeep, sel, neg)

    picked = jnp.zeros(shape3, F32)
    for _ in range(TOP_K):
        m = jnp.max(jnp.max(sel, axis=1, keepdims=True), axis=0, keepdims=True)
        cand = jnp.where(sel == m, e_idx, float(N_EXPERTS))
        pick = jnp.min(jnp.min(cand, axis=1, keepdims=True), axis=0, keepdims=True)
        hit = e_idx == pick
        picked = jnp.where(hit, s, picked)
        sel = jnp.where(hit, neg, sel)
    total = jnp.sum(jnp.sum(picked, axis=1, keepdims=True), axis=0, keepdims=True)
    gate = picked / total * ROUTED_SCALE
    gate = gate.reshape(N_EXPERTS, tn)
    gate_ref[...] = jnp.concatenate([gate, jnp.zeros_like(gate)], axis=0).T


def _route(logits, bias, tn):
    n, w = logits.shape
    e = bias.shape[0]
    return pl.pallas_call(
        _route_kernel,
        grid=(n // tn,),
        in_specs=[pl.BlockSpec((tn, w), lambda i: (i, 0)),
                  pl.BlockSpec((e, 1), lambda i: (0, 0))],
        out_specs=pl.BlockSpec((tn, w), lambda i: (i, 0)),
        out_shape=jax.ShapeDtypeStruct((n, w), F32),
        compiler_params=_params(("parallel",)),
        name="route",
    )(logits, bias.reshape(e, 1))


def _moe_kernel(u_ref, gate_ref, x1_ref, g2_ref, wg_ref, wu_ref, wd_ref, sg_ref, su_ref, sd_ref,
                lg_ref, lb_ref, o_ref, acc_scr, *, eg):
    j = pl.program_id(1)
    u = u_ref[...]

    @pl.when(j == 0)
    def _():
        hs = _silu(jnp.dot(u, sg_ref[...], preferred_element_type=F32)) * jnp.dot(
            u, su_ref[...], preferred_element_type=F32)
        acc_scr[...] = jnp.dot(hs.astype(BF16), sd_ref[...], preferred_element_type=F32)

    gate = gate_ref[...]
    lane = lax.broadcasted_iota(jnp.int32, gate.shape, 1)
    acts = []
    for el in range(eg):
        gk = jnp.sum(jnp.where(lane == j * eg + el, gate, 0.0), axis=-1, keepdims=True)
        hg = jnp.dot(u, wg_ref[el], preferred_element_type=F32)
        hu = jnp.dot(u, wu_ref[el], preferred_element_type=F32)
        acts.append((_silu(hg) * hu * gk).astype(BF16))
    act = jnp.concatenate(acts, axis=-1)
    acc_scr[...] += jnp.dot(act, wd_ref[0], preferred_element_type=F32)

    @pl.when(j == pl.num_programs(1) - 1)
    def _():
        z = ALPHA * x1_ref[...] + g2_ref[0] * acc_scr[...]
        o_ref[...] = _ln0(z) * lg_ref[...] + lb_ref[...]


def _moe(u, gate, x1, g2, wg, wu, wd, sg, su, sd, ln_g, ln_b, seq, tm, eg):
    n, d = x1.shape
    n_e, _, f = wg.shape
    per_b = seq // tm
    wd = wd.reshape(n_e // eg, eg * f, d)
    row = pl.BlockSpec((tm, d), lambda i, j: (i, 0))
    vec = pl.BlockSpec((1, d), lambda i, j: (0, 0))
    return pl.pallas_call(
        functools.partial(_moe_kernel, eg=eg),
        grid=(n // tm, n_e // eg),
        in_specs=[row, pl.BlockSpec((tm, gate.shape[1]), lambda i, j: (i, 0)), row,
                  pl.BlockSpec((1, 1, d), lambda i, j: (i // per_b, 0, 0)),
                  pl.BlockSpec((eg, d, f), lambda i, j: (j, 0, 0)),
                  pl.BlockSpec((eg, d, f), lambda i, j: (j, 0, 0)),
                  pl.BlockSpec((1, eg * f, d), lambda i, j: (j, 0, 0)),
                  pl.BlockSpec(sg.shape, lambda i, j: (0, 0)),
                  pl.BlockSpec(su.shape, lambda i, j: (0, 0)),
                  pl.BlockSpec(sd.shape, lambda i, j: (0, 0)),
                  vec, vec],
        out_specs=row,
        out_shape=jax.ShapeDtypeStruct((n, d), F32),
        scratch_shapes=[pltpu.VMEM((tm, d), F32)],
        compiler_params=_params(("parallel", "arbitrary")),
        name="moe",
    )(u, gate, x1, g2, wg, wu, wd, sg, su, sd, ln_g, ln_b)


def _block_diag_tiles(w):
    nb, bw, _ = w.shape
    per = MXU_TILE // bw
    w = w.reshape(nb // per, per, bw, bw)
    eye = jnp.eye(per, dtype=w.dtype)
    tiles = jnp.einsum("gpij,pq->gpiqj", w, eye)
    return tiles.reshape(nb // per, MXU_TILE, MXU_TILE)


def kernel(x, c, w_ada, b_ada, w_in, conv_w, conv_b, lru_w_a, lru_b_a, lru_w_i, lru_b_i, lru_lambda,
           lam_q1, lam_k1, lam_q2, lam_k2, subln_g, p_rnn, p_attn, w_out, ln1_g, ln1_b,
           w_router, router_bias, we_gate, we_up, we_down, ws_gate, ws_up, ws_down, ln2_g, ln2_b):
    bsz, seq, d = x.shape
    n = bsz * seq
    x2 = x.reshape(n, d)
    l = 0

    ada = _ada(c, w_ada[l], b_ada[l])
    sh1, sc1, g1, sh2, sc2, g2 = [ada[:, i * d:(i + 1) * d].reshape(bsz, 1, d) for i in range(6)]

    proj = _inproj(x2, sh1, sc1, w_in[l], seq, tm=2048)

    ma = _rnn(proj, conv_w[l], conv_b[l].reshape(1, d),
              _block_diag_tiles(lru_w_a[l]).astype(BF16), lru_b_a[l].reshape(1, d),
              _block_diag_tiles(lru_w_i[l]).astype(BF16), lru_b_i[l].reshape(1, d),
              lru_lambda[l].reshape(1, d), p_rnn[l].astype(BF16), bsz, seq, tt=512)

    yb = _attn(proj, lam_q1[l].reshape(1, -1), lam_k1[l].reshape(1, -1),
               lam_q2[l].reshape(1, -1), lam_k2[l].reshape(1, -1),
               subln_g[l].reshape(1, -1), bsz, seq, tq=256)

    wr_hi = w_router[l].astype(BF16)
    wr_lo = (w_router[l] - wr_hi.astype(F32)).astype(BF16)
    x1, u, logits = _postmix(yb, ma, proj, x2, g1, sh2, sc2, ln1_g[l].reshape(1, d),
                             ln1_b[l].reshape(1, d), p_attn[l].astype(BF16),
                             w_out[l].astype(BF16), jnp.concatenate([wr_hi, wr_lo], axis=1), seq, tm=1024)

    gate = _route(logits, router_bias[l], tn=1024)

    out = _moe(u, gate, x1, g2, we_gate[l].astype(BF16), we_up[l].astype(BF16),
               we_down[l].astype(BF16), ws_gate[l].astype(BF16), ws_up[l].astype(BF16),
               ws_down[l].astype(BF16), ln2_g[l].reshape(1, d), ln2_b[l].reshape(1, d),
               seq, tm=1024, eg=4)
    return out.reshape(bsz, seq, d)
```

```python
import functools
import math

import jax
import jax.numpy as jnp
from jax import lax
from jax.experimental import pallas as pl
from jax.experimental.pallas import tpu as pltpu

F32 = jnp.float32
BF16 = jnp.bfloat16

D_MODEL = 1024
CHUNK = 64
RNN_BLOCKS = 16
CONV_W = 4
LRU_C = 8.0
N_HEADS = 8
DIFF_DH = 64
HEAD_W = 2 * DIFF_DH
N_EXPERTS = 64
TOP_K = 8
N_GROUPS = 8
GROUP_SIZE = N_EXPERTS // N_GROUPS
TOPK_GROUPS = 4
D_EXPERT = 256
ROUTED_SCALE = 2.5
DEPTH = 1
ALPHA = (2.0 * DEPTH) ** 0.25
LN_EPS = 1e-5
RMS_EPS = 1e-5
LAM_INIT = 0.8 - 0.6 * math.exp(-0.3 * 0)

COL_X, COL_Y, COL_Q, COL_K, COL_V, COL_GR, COL_GA = range(7)
N_COLS = 7

V7X_VMEM_LIMIT = 56 * 1024 * 1024
MXU_TILE = 256
SUBLANES = 8
LANES = 128
POSTMIX_CHUNK = 512

TILES = dict(
    inproj_rows=2048,
    rnn_rows=512,
    attn_q_rows=256,
    postmix_rows=1024,
    route_tokens=1024,
    moe_rows=1024,
    moe_experts=5,
)


def _ln0(x):
    mu = jnp.mean(x, axis=-1, keepdims=True)
    xc = x - mu
    var = jnp.mean(xc * xc, axis=-1, keepdims=True)
    return xc * lax.rsqrt(var + LN_EPS)


def _silu(x):
    return x * jax.nn.sigmoid(x)


def _params(sem):
    return pltpu.CompilerParams(dimension_semantics=sem, vmem_limit_bytes=V7X_VMEM_LIMIT)


def _ada_kernel(c_ref, w_ref, b_ref, o_ref):
    cond = _silu(c_ref[...])
    o_ref[...] = jnp.dot(cond, w_ref[...], preferred_element_type=F32,
                         precision=lax.Precision.HIGHEST) + b_ref[...]


def _ada(c, w, b):
    bsz, d = c.shape
    n_out = w.shape[1]
    return pl.pallas_call(
        _ada_kernel,
        grid=(n_out // d,),
        in_specs=[pl.BlockSpec((bsz, d), lambda j: (0, 0)),
                  pl.BlockSpec((d, d), lambda j: (0, j)),
                  pl.BlockSpec((1, d), lambda j: (0, j))],
        out_specs=pl.BlockSpec((bsz, d), lambda j: (0, j)),
        out_shape=jax.ShapeDtypeStruct((bsz, n_out), F32),
        compiler_params=_params(("parallel",)),
        name="ada",
    )(c, w, b.reshape(1, n_out))


def _inproj_kernel(x_ref, sh_ref, sc_ref, w_ref, o_ref, h_scr):
    j = pl.program_id(1)

    @pl.when(j == 0)
    def _():
        h = _ln0(x_ref[...]) * (1.0 + sc_ref[0]) + sh_ref[0]
        h_scr[...] = h.astype(BF16)

    acc = jnp.dot(h_scr[...], w_ref[...].astype(BF16), preferred_element_type=F32)
    o_ref[...] = jnp.where(j >= COL_GR, jax.nn.sigmoid(acc), acc).astype(BF16)


def _inproj(x2, sh, sc, w, seq, tm):
    n, d = x2.shape
    per_b = seq // tm
    return pl.pallas_call(
        _inproj_kernel,
        grid=(n // tm, N_COLS),
        in_specs=[pl.BlockSpec((tm, d), lambda i, j: (i, 0)),
                  pl.BlockSpec((1, 1, d), lambda i, j: (i // per_b, 0, 0)),
                  pl.BlockSpec((1, 1, d), lambda i, j: (i // per_b, 0, 0)),
                  pl.BlockSpec((d, d), lambda i, j: (0, j))],
        out_specs=pl.BlockSpec((tm, d), lambda i, j: (i, j)),
        out_shape=jax.ShapeDtypeStruct((n, N_COLS * d), BF16),
        scratch_shapes=[pltpu.VMEM((tm, d), BF16)],
        compiler_params=_params(("parallel", "arbitrary")),
        name="inproj",
    )(x2, sh, sc, w)


def _stream_pitch(n_slab):
    assert n_slab % SUBLANES == 0
    return n_slab if (n_slab // SUBLANES) % 2 else n_slab + SUBLANES


def _rnn_kernel(xr_ref, yr_ref, g_ref, cw_ref, cb_ref, wa_ref, ba_ref, wi_ref, bi_ref,
                lam_ref, p_ref, o_ref, x_scr, y_scr, g_scr, o_scr, tail_scr, h_scr):
    t = pl.program_id(1)
    tt, d = xr_ref.shape
    n_slab = tt // SUBLANES
    n_hist = CONV_W - 1

    @pl.when(t == 0)
    def _():
        tail_scr[...] = jnp.zeros_like(tail_scr)
        h_scr[...] = jnp.zeros_like(h_scr)

    def slab(v, j):
        return v[j * SUBLANES:(j + 1) * SUBLANES, :]

    n_lane = d // LANES
    pitch = _stream_pitch(n_slab)

    def permuted(src_ref, scr):
        x = src_ref[...].astype(F32)
        for c in range(n_lane):
            for s in range(SUBLANES):
                scr[c, s * pitch:s * pitch + n_slab, :] = x[s * n_slab:(s + 1) * n_slab, c * LANES:(c + 1) * LANES]
        return [jnp.concatenate([scr[c, pl.ds(j, SUBLANES, stride=pitch), :] for c in range(n_lane)],
                                axis=-1) for j in range(n_slab)]

    sub = lax.broadcasted_iota(jnp.int32, (SUBLANES, d), 0)

    xs = permuted(xr_ref, x_scr)
    wrapped = []
    for i in range(n_hist):
        cur = xs[n_slab - n_hist + i]
        prev = slab(tail_scr, i)
        wrapped.append(pltpu.roll(jnp.where(sub == SUBLANES - 1, prev, cur), 1, 0))
        tail_scr[i * SUBLANES:(i + 1) * SUBLANES, :] = cur

    def x_at(j):
        return xs[j] if j >= 0 else wrapped[j + n_hist]

    cw = cw_ref[...]
    w = [cw[i:i + 1, :] for i in range(CONV_W)]
    cb = cb_ref[...]
    xc = jnp.concatenate(
        [cb + w[3] * xs[j] + w[2] * x_at(j - 1) + w[1] * x_at(j - 2) + w[0] * x_at(j - 3)
         for j in range(n_slab)], axis=0)

    xcb = xc.astype(BF16)
    n_g = d // MXU_TILE
    ra = jnp.concatenate(
        [jnp.dot(xcb[:, g * MXU_TILE:(g + 1) * MXU_TILE], wa_ref[g], preferred_element_type=F32)
         for g in range(n_g)], axis=-1)
    ri = jnp.concatenate(
        [jnp.dot(xcb[:, g * MXU_TILE:(g + 1) * MXU_TILE], wi_ref[g], preferred_element_type=F32)
         for g in range(n_g)], axis=-1)
    r = jax.nn.sigmoid(ra + ba_ref[...])
    gi = jax.nn.sigmoid(ri + bi_ref[...])
    z = -lam_ref[...]
    softplus = jnp.maximum(z, 0.0) + jnp.log1p(jnp.exp(-jnp.abs(z)))
    log_a = (-LRU_C) * r * softplus
    a = jnp.exp(log_a)
    mult = jnp.sqrt(jnp.maximum(1.0 - a * a, 0.0))
    u = xc * gi * mult

    h_loc, p_loc = [slab(u, 0)], [slab(a, 0)]
    for j in range(1, n_slab):
        aj = slab(a, j)
        h_loc.append(aj * h_loc[-1] + slab(u, j))
        p_loc.append(aj * p_loc[-1])
    pa, hb = p_loc[-1], h_loc[-1]
    step = 1
    while step < SUBLANES:
        valid = sub >= step
        hb = jnp.where(valid, pa * pltpu.roll(hb, step, 0) + hb, hb)
        pa = jnp.where(valid, pa * pltpu.roll(pa, step, 0), pa)
        step *= 2
    carry = h_scr[...]
    end = pa * carry + hb
    start = jnp.where(sub == 0, carry, pltpu.roll(end, 1, 0))
    h_scr[...] = end[SUBLANES - 1:SUBLANES, :]
    h = jnp.concatenate([h_loc[j] + p_loc[j] * start for j in range(n_slab)], axis=0)

    ys = jnp.concatenate(permuted(yr_ref, y_scr), axis=0)
    gs = jnp.concatenate(permuted(g_ref, g_scr), axis=0)
    ya = (h * jax.nn.gelu(ys)).astype(BF16)
    out = gs * jnp.dot(ya, p_ref[...], preferred_element_type=F32)
    for j in range(n_slab):
        for c in range(n_lane):
            o_scr[c, pl.ds(j, SUBLANES, stride=pitch), :] = slab(out, j)[:, c * LANES:(c + 1) * LANES]
    o_ref[...] = jnp.concatenate(
        [jnp.concatenate([o_scr[c, s * pitch:s * pitch + n_slab, :] for s in range(SUBLANES)], axis=0)
         for c in range(n_lane)], axis=-1).astype(BF16)


def _rnn(proj, cw, cb, wa, ba, wi, bi, lam, p_rnn, bsz, seq, tt):
    n = proj.shape[0]
    d = D_MODEL
    per_b = seq // tt
    row = lambda b, t: b * per_b + t
    vec = pl.BlockSpec((1, d), lambda b, t: (0, 0))
    gate_w = pl.BlockSpec(wa.shape, lambda b, t: (0, 0, 0))
    tile = pltpu.VMEM((d // LANES, SUBLANES * _stream_pitch(tt // SUBLANES), LANES), F32)
    return pl.pallas_call(
        _rnn_kernel,
        grid=(bsz, per_b),
        in_specs=[pl.BlockSpec((tt, d), lambda b, t: (row(b, t), COL_X)),
                  pl.BlockSpec((tt, d), lambda b, t: (row(b, t), COL_Y)),
                  pl.BlockSpec((tt, d), lambda b, t: (row(b, t), COL_GR)),
                  pl.BlockSpec((CONV_W, d), lambda b, t: (0, 0)),
                  vec, gate_w, vec, gate_w, vec, vec,
                  pl.BlockSpec((d, d), lambda b, t: (0, 0))],
        out_specs=pl.BlockSpec((tt, d), lambda b, t: (row(b, t), 0)),
        out_shape=jax.ShapeDtypeStruct((n, d), BF16),
        scratch_shapes=[tile, tile, tile, tile,
                        pltpu.VMEM(((CONV_W - 1) * SUBLANES, d), F32), pltpu.VMEM((1, d), F32)],
        compiler_params=_params(("parallel", "arbitrary")),
        name="rnn",
    )(proj, proj, proj, cw, cb, wa, ba, wi, bi, lam, p_rnn)


def _attn_kernel(q_ref, k_ref, v_ref, lq1_ref, lk1_ref, lq2_ref, lk2_ref, sg_ref, o_ref, *, tq):
    seq = k_ref.shape[0]
    lam = (jnp.exp(jnp.sum(lq1_ref[...] * lk1_ref[...], axis=-1, keepdims=True))
           - jnp.exp(jnp.sum(lq2_ref[...] * lk2_ref[...], axis=-1, keepdims=True)) + LAM_INIT)
    nt = (((1,), (1,)), ((), ()))
    scale = DIFF_DH ** -0.5
    lane = lax.broadcasted_iota(jnp.int32, (tq, HEAD_W), 1)
    diag = (lax.broadcasted_iota(jnp.int32, (tq, tq), 1) // CHUNK
            <= lax.broadcasted_iota(jnp.int32, (tq, tq), 0) // CHUNK)

    diag2 = jnp.concatenate([diag, diag], axis=0)

    for qi in reversed(range(seq // tq)):
        kv = (qi + 1) * tq
        q = q_ref[qi * tq:(qi + 1) * tq, :] * scale
        zero = jnp.zeros_like(q)
        qq = jnp.concatenate([jnp.where(lane < DIFF_DH, q, zero), jnp.where(lane >= DIFF_DH, q, zero)], axis=0)
        s = lax.dot_general(qq, k_ref[0:kv, :], nt, preferred_element_type=F32)
        s_diag = jnp.where(diag2, s[:, kv - tq:], -jnp.inf)
        s = s_diag if kv == tq else jnp.concatenate([s[:, :kv - tq], s_diag], axis=1)
        e = jnp.exp(s - jnp.max(s, axis=-1, keepdims=True))
        pv = jnp.dot(e.astype(BF16), v_ref[0:kv, :], preferred_element_type=F32)
        pv = pv / jnp.sum(e, axis=-1, keepdims=True)
        o = pv[:tq] - lam * pv[tq:]
        o = o * lax.rsqrt(jnp.mean(o * o, axis=-1, keepdims=True) + RMS_EPS) * sg_ref[...]
        o_ref[qi * tq:(qi + 1) * tq, :] = (o * (1.0 - LAM_INIT)).astype(BF16)


def _attn(proj, lq1, lk1, lq2, lk2, sg, bsz, seq, tq):
    n = proj.shape[0]
    hb = D_MODEL // HEAD_W
    lam_spec = pl.BlockSpec((1, DIFF_DH), lambda b, h: (0, 0))
    return pl.pallas_call(
        functools.partial(_attn_kernel, tq=tq),
        grid=(bsz, N_HEADS),
        in_specs=[pl.BlockSpec((seq, HEAD_W), lambda b, h: (b, COL_Q * hb + h)),
                  pl.BlockSpec((seq, HEAD_W), lambda b, h: (b, COL_K * hb + h)),
                  pl.BlockSpec((seq, HEAD_W), lambda b, h: (b, COL_V * hb + h)),
                  lam_spec, lam_spec, lam_spec, lam_spec,
                  pl.BlockSpec((1, HEAD_W), lambda b, h: (0, 0))],
        out_specs=pl.BlockSpec((seq, HEAD_W), lambda b, h: (b, h)),
        out_shape=jax.ShapeDtypeStruct((n, D_MODEL), BF16),
        compiler_params=_params(("parallel", "parallel")),
        name="attn",
    )(proj, proj, proj, lq1, lk1, lq2, lk2, sg)


def _postmix_kernel(yb_ref, ma_ref, ga_ref, x_ref, g1_ref, sh_ref, sc_ref, lg_ref, lb_ref,
                    pa_ref, wo_ref, wr_ref, x1_ref, u_ref, lt_ref):
    tm = x_ref.shape[0]

    def project(rows):
        branch_b = jnp.dot(yb_ref[rows, :], pa_ref[...], preferred_element_type=F32)
        merged = ma_ref[rows, :].astype(F32) + ga_ref[rows, :].astype(F32) * branch_b
        return jnp.dot(merged.astype(BF16), wo_ref[...], preferred_element_type=F32)

    def normalize(rows, mix):
        x1 = _ln0(ALPHA * x_ref[rows, :] + g1_ref[0] * mix) * lg_ref[...] + lb_ref[...]
        x1_ref[rows, :] = x1
        u = _ln0(x1) * (1.0 + sc_ref[0]) + sh_ref[0]
        u_ref[rows, :] = u.astype(BF16)
        u_hi = u.astype(BF16)
        u_lo = (u - u_hi.astype(F32)).astype(BF16)
        a = jnp.dot(u_hi, wr_ref[...], preferred_element_type=F32)
        b = jnp.dot(u_lo, wr_ref[...], preferred_element_type=F32)
        lt_ref[rows, :] = a + pltpu.roll(a, N_EXPERTS, 1) + b

    chunks = [slice(r0, r0 + POSTMIX_CHUNK) for r0 in range(0, tm, POSTMIX_CHUNK)]
    mix = project(chunks[0])
    for prev, rows in zip(chunks[:-1], chunks[1:]):
        nxt = project(rows)
        normalize(prev, mix)
        mix = nxt
    normalize(chunks[-1], mix)


def _postmix(yb, ma, proj, x2, g1, sh2, sc2, ln_g, ln_b, p_attn, w_out, wr_hilo, seq, tm):
    n, d = x2.shape
    per_b = seq // tm
    row = pl.BlockSpec((tm, d), lambda i: (i, 0))
    mod = pl.BlockSpec((1, 1, d), lambda i: (i // per_b, 0, 0))
    vec = pl.BlockSpec((1, d), lambda i: (0, 0))
    mat = pl.BlockSpec((d, d), lambda i: (0, 0))
    return pl.pallas_call(
        _postmix_kernel,
        grid=(n // tm,),
        in_specs=[row, row, pl.BlockSpec((tm, d), lambda i: (i, COL_GA)), row,
                  mod, mod, mod, vec, vec, mat, mat,
                  pl.BlockSpec((d, 2 * N_EXPERTS), lambda i: (0, 0))],
        out_specs=[row, row, pl.BlockSpec((tm, 2 * N_EXPERTS), lambda i: (i, 0))],
        out_shape=[jax.ShapeDtypeStruct((n, d), F32),
                   jax.ShapeDtypeStruct((n, d), BF16),
                   jax.ShapeDtypeStruct((n, 2 * N_EXPERTS), F32)],
        compiler_params=_params(("parallel",)),
        name="postmix",
    )(yb, ma, proj, x2, g1, sh2, sc2, ln_g, ln_b, p_attn, w_out, wr_hilo)


def _route_kernel(lt_ref, bias_ref, gate_ref):
    tn = lt_ref.shape[0]
    shape3 = (N_GROUPS, GROUP_SIZE, tn)
    neg = -jnp.inf
    s = jax.nn.sigmoid(lt_ref[...].T[:N_EXPERTS, :])
    sel = (s + bias_ref[...]).reshape(shape3)
    s = s.reshape(shape3)
    j_idx = lax.broadcasted_iota(jnp.int32, shape3, 1).astype(F32)
    g_idx = lax.broadcasted_iota(jnp.int32, shape3, 0).astype(F32)
    e_idx = g_idx * GROUP_SIZE + j_idx

    m1 = jnp.max(sel, axis=1, keepdims=True)
    first = jnp.min(jnp.where(sel == m1, j_idx, float(GROUP_SIZE)), axis=1, keepdims=True)
    m2 = jnp.max(jnp.where(j_idx == first, neg, sel), axis=1, keepdims=True)
    gscore = m1 + m2

    gi = lax.broadcasted_iota(jnp.int32, gscore.shape, 0).astype(F32)
    gkeep = jnp.zeros(gscore.shape, jnp.bool_)
    for _ in range(TOPK_GROUPS):
        m = jnp.max(gscore, axis=0, keepdims=True)
        pick = jnp.min(jnp.where(gscore == m, gi, float(N_GROUPS)), axis=0, keepdims=True)
        hit = gi == pick
        gkeep = jnp.logical_or(gkeep, hit)
        gscore = jnp.where(hit, neg, gscore)
    sel = jnp.where(gkeep, sel, neg)

    picked = jnp.zeros(shape3, F32)
    for _ in range(TOP_K):
        m = jnp.max(jnp.max(sel, axis=1, keepdims=True), axis=0, keepdims=True)
        cand = jnp.where(sel == m, e_idx, float(N_EXPERTS))
        pick = jnp.min(jnp.min(cand, axis=1, keepdims=True), axis=0, keepdims=True)
        hit = e_idx == pick
        picked = jnp.where(hit, s, picked)
        sel = jnp.where(hit, neg, sel)
    total = jnp.sum(jnp.sum(picked, axis=1, keepdims=True), axis=0, keepdims=True)
    gate = picked / total * ROUTED_SCALE
    gate = gate.reshape(N_EXPERTS, tn)
    shared = (lax.broadcasted_iota(jnp.int32, gate.shape, 0) == 0).astype(F32)
    gate_ref[...] = jnp.concatenate([gate, shared], axis=0).T


def _route(logits, bias, tn):
    n, w = logits.shape
    e = bias.shape[0]
    return pl.pallas_call(
        _route_kernel,
        grid=(n // tn,),
        in_specs=[pl.BlockSpec((tn, w), lambda i: (i, 0)),
                  pl.BlockSpec((e, 1), lambda i: (0, 0))],
        out_specs=pl.BlockSpec((tn, w), lambda i: (i, 0)),
        out_shape=jax.ShapeDtypeStruct((n, w), F32),
        compiler_params=_params(("parallel",)),
        name="route",
    )(logits, bias.reshape(e, 1))


def _moe_kernel(u_ref, gate_ref, x1_ref, g2_ref, wg_ref, wu_ref, wd_ref, lg_ref, lb_ref, o_ref, acc_scr,
                *, eg):
    j = pl.program_id(1)

    @pl.when(j == 0)
    def _():
        acc_scr[...] = jnp.zeros_like(acc_scr)

    u = u_ref[...]
    gate = gate_ref[...]
    lane = lax.broadcasted_iota(jnp.int32, gate.shape, 1)
    acts = []
    for el in range(eg):
        gk = jnp.sum(jnp.where(lane == j * eg + el, gate, 0.0), axis=-1, keepdims=True)
        hg = jnp.dot(u, wg_ref[el], preferred_element_type=F32)
        hu = jnp.dot(u, wu_ref[el], preferred_element_type=F32)
        acts.append((_silu(hg) * hu * gk).astype(BF16))
    acc_scr[...] += jnp.dot(jnp.concatenate(acts, axis=-1), wd_ref[0], preferred_element_type=F32)

    @pl.when(j == pl.num_programs(1) - 1)
    def _():
        z = ALPHA * x1_ref[...] + g2_ref[0] * acc_scr[...]
        o_ref[...] = _ln0(z) * lg_ref[...] + lb_ref[...]


def _moe(u, gate, x1, g2, wg, wu, wd, ln_g, ln_b, seq, tm, eg):
    n, d = x1.shape
    n_e, _, f = wg.shape
    assert n_e % eg == 0 and n_e <= gate.shape[1]
    per_b = seq // tm
    wd = wd.reshape(n_e // eg, eg * f, d)
    row = pl.BlockSpec((tm, d), lambda i, j: (i, 0))
    vec = pl.BlockSpec((1, d), lambda i, j: (0, 0))
    return pl.pallas_call(
        functools.partial(_moe_kernel, eg=eg),
        grid=(n // tm, n_e // eg),
        in_specs=[row, pl.BlockSpec((tm, gate.shape[1]), lambda i, j: (i, 0)), row,
                  pl.BlockSpec((1, 1, d), lambda i, j: (i // per_b, 0, 0)),
                  pl.BlockSpec((eg, d, f), lambda i, j: (j, 0, 0)),
                  pl.BlockSpec((eg, d, f), lambda i, j: (j, 0, 0)),
                  pl.BlockSpec((1, eg * f, d), lambda i, j: (j, 0, 0)),
                  vec, vec],
        out_specs=row,
        out_shape=jax.ShapeDtypeStruct((n, d), F32),
        scratch_shapes=[pltpu.VMEM((tm, d), F32)],
        compiler_params=_params(("parallel", "arbitrary")),
        name="moe",
    )(u, gate, x1, g2, wg, wu, wd, ln_g, ln_b)


def _block_diag_tiles(w):
    nb, bw, _ = w.shape
    per = MXU_TILE // bw
    w = w.reshape(nb // per, per, bw, bw)
    eye = jnp.eye(per, dtype=w.dtype)
    tiles = jnp.einsum("gpij,pq->gpiqj", w, eye)
    return tiles.reshape(nb // per, MXU_TILE, MXU_TILE)


def kernel(x, c, w_ada, b_ada, w_in, conv_w, conv_b, lru_w_a, lru_b_a, lru_w_i, lru_b_i, lru_lambda,
           lam_q1, lam_k1, lam_q2, lam_k2, subln_g, p_rnn, p_attn, w_out, ln1_g, ln1_b,
           w_router, router_bias, we_gate, we_up, we_down, ws_gate, ws_up, ws_down, ln2_g, ln2_b):
    bsz, seq, d = x.shape
    n = bsz * seq
    x2 = x.reshape(n, d)
    l = 0

    ada = _ada(c, w_ada[l], b_ada[l])
    sh1, sc1, g1, sh2, sc2, g2 = [ada[:, i * d:(i + 1) * d].reshape(bsz, 1, d) for i in range(6)]

    proj = _inproj(x2, sh1, sc1, w_in[l], seq, tm=TILES["inproj_rows"])

    ma = _rnn(proj, conv_w[l], conv_b[l].reshape(1, d),
              _block_diag_tiles(lru_w_a[l]).astype(BF16), lru_b_a[l].reshape(1, d),
              _block_diag_tiles(lru_w_i[l]).astype(BF16), lru_b_i[l].reshape(1, d),
              lru_lambda[l].reshape(1, d), p_rnn[l].astype(BF16), bsz, seq, tt=TILES["rnn_rows"])

    yb = _attn(proj, lam_q1[l].reshape(1, -1), lam_k1[l].reshape(1, -1),
               lam_q2[l].reshape(1, -1), lam_k2[l].reshape(1, -1),
               subln_g[l].reshape(1, -1), bsz, seq, tq=TILES["attn_q_rows"])

    wr_hi = w_router[l].astype(BF16)
    wr_lo = (w_router[l] - wr_hi.astype(F32)).astype(BF16)
    x1, u, logits = _postmix(yb, ma, proj, x2, g1, sh2, sc2, ln1_g[l].reshape(1, d),
                             ln1_b[l].reshape(1, d), p_attn[l].astype(BF16),
                             w_out[l].astype(BF16), jnp.concatenate([wr_hi, wr_lo], axis=1), seq,
                             tm=TILES["postmix_rows"])

    gate = _route(logits, router_bias[l], tn=TILES["route_tokens"])

    with_shared = lambda we, ws: jnp.concatenate([we, ws[None]], axis=0).astype(BF16)
    out = _moe(u, gate, x1, g2, with_shared(we_gate[l], ws_gate[l]), with_shared(we_up[l], ws_up[l]),
               with_shared(we_down[l], ws_down[l]), ln2_g[l].reshape(1, d), ln2_b[l].reshape(1, d),
               seq, tm=TILES["moe_rows"], eg=TILES["moe_experts"])
    return out.reshape(bsz, seq, d)
```

```python
import functools
import math

import jax
import jax.numpy as jnp
from jax import lax
from jax.experimental import pallas as pl
from jax.experimental.pallas import tpu as pltpu

F32 = jnp.float32
BF16 = jnp.bfloat16

D_MODEL = 1024
CHUNK = 64
RNN_BLOCKS = 16
CONV_W = 4
LRU_C = 8.0
N_HEADS = 8
DIFF_DH = 64
HEAD_W = 2 * DIFF_DH
N_EXPERTS = 64
TOP_K = 8
N_GROUPS = 8
GROUP_SIZE = N_EXPERTS // N_GROUPS
TOPK_GROUPS = 4
D_EXPERT = 256
ROUTED_SCALE = 2.5
DEPTH = 1
ALPHA = (2.0 * DEPTH) ** 0.25
LN_EPS = 1e-5
RMS_EPS = 1e-5
LAM_INIT = 0.8 - 0.6 * math.exp(-0.3 * 0)

COL_X, COL_Y, COL_Q, COL_K, COL_V, COL_GR, COL_GA = range(7)
N_COLS = 7

V7X_VMEM_LIMIT = 56 * 1024 * 1024
MXU_TILE = 256
SUBLANES = 8
LANES = 128
POSTMIX_CHUNK = 512

TILES = dict(
    inproj_rows=2048,
    rnn_rows=512,
    attn_q_rows=256,
    postmix_rows=1024,
    route_tokens=1024,
    moe_rows=1024,
    moe_experts=5,
)


def _ln0(x):
    mu = jnp.mean(x, axis=-1, keepdims=True)
    xc = x - mu
    var = jnp.mean(xc * xc, axis=-1, keepdims=True)
    return xc * lax.rsqrt(var + LN_EPS)


def _silu(x):
    return x * jax.nn.sigmoid(x)


def _params(sem):
    return pltpu.CompilerParams(dimension_semantics=sem, vmem_limit_bytes=V7X_VMEM_LIMIT)


def _ada_kernel(c_ref, w_ref, b_ref, o_ref):
    cond = _silu(c_ref[...])
    o_ref[...] = jnp.dot(cond, w_ref[...], preferred_element_type=F32,
                         precision=lax.Precision.HIGHEST) + b_ref[...]


def _ada(c, w, b):
    bsz, d = c.shape
    n_out = w.shape[1]
    return pl.pallas_call(
        _ada_kernel,
        grid=(n_out // d,),
        in_specs=[pl.BlockSpec((bsz, d), lambda j: (0, 0)),
                  pl.BlockSpec((d, d), lambda j: (0, j)),
                  pl.BlockSpec((1, d), lambda j: (0, j))],
        out_specs=pl.BlockSpec((bsz, d), lambda j: (0, j)),
        out_shape=jax.ShapeDtypeStruct((bsz, n_out), F32),
        compiler_params=_params(("parallel",)),
        name="ada",
    )(c, w, b.reshape(1, n_out))


def _inproj_kernel(x_ref, sh_ref, sc_ref, w_ref, o_ref, h_scr):
    j = pl.program_id(1)

    @pl.when(j == 0)
    def _():
        h = _ln0(x_ref[...]) * (1.0 + sc_ref[0]) + sh_ref[0]
        h_scr[...] = h.astype(BF16)

    acc = jnp.dot(h_scr[...], w_ref[...].astype(BF16), preferred_element_type=F32)
    o_ref[...] = jnp.where(j >= COL_GR, jax.nn.sigmoid(acc), acc).astype(BF16)


def _inproj(x2, sh, sc, w, seq, tm):
    n, d = x2.shape
    per_b = seq // tm
    return pl.pallas_call(
        _inproj_kernel,
        grid=(n // tm, N_COLS),
        in_specs=[pl.BlockSpec((tm, d), lambda i, j: (i, 0)),
                  pl.BlockSpec((1, 1, d), lambda i, j: (i // per_b, 0, 0)),
                  pl.BlockSpec((1, 1, d), lambda i, j: (i // per_b, 0, 0)),
                  pl.BlockSpec((d, d), lambda i, j: (0, j))],
        out_specs=pl.BlockSpec((tm, d), lambda i, j: (i, j)),
        out_shape=jax.ShapeDtypeStruct((n, N_COLS * d), BF16),
        scratch_shapes=[pltpu.VMEM((tm, d), BF16)],
        compiler_params=_params(("parallel", "arbitrary")),
        name="inproj",
    )(x2, sh, sc, w)


def _stream_pitch(n_slab):
    assert n_slab % SUBLANES == 0
    return n_slab if (n_slab // SUBLANES) % 2 else n_slab + SUBLANES


def _rnn_kernel(xr_ref, yr_ref, g_ref, cw_ref, cb_ref, wa_ref, ba_ref, wi_ref, bi_ref,
                lam_ref, p_ref, o_ref, x_scr, y_scr, g_scr, o_scr, tail_scr, h_scr):
    t = pl.program_id(1)
    tt, d = xr_ref.shape
    n_slab = tt // SUBLANES
    n_hist = CONV_W - 1

    @pl.when(t == 0)
    def _():
        tail_scr[...] = jnp.zeros_like(tail_scr)
        h_scr[...] = jnp.zeros_like(h_scr)

    def slab(v, j):
        return v[j * SUBLANES:(j + 1) * SUBLANES, :]

    n_lane = d // LANES
    pitch = _stream_pitch(n_slab)

    def permuted(src_ref, scr):
        x = src_ref[...].astype(F32)
        for c in range(n_lane):
            for s in range(SUBLANES):
                scr[c, s * pitch:s * pitch + n_slab, :] = x[s * n_slab:(s + 1) * n_slab, c * LANES:(c + 1) * LANES]
        return [jnp.concatenate([scr[c, pl.ds(j, SUBLANES, stride=pitch), :] for c in range(n_lane)],
                                axis=-1) for j in range(n_slab)]

    sub = lax.broadcasted_iota(jnp.int32, (SUBLANES, d), 0)

    xs = permuted(xr_ref, x_scr)
    wrapped = []
    for i in range(n_hist):
        cur = xs[n_slab - n_hist + i]
        prev = slab(tail_scr, i)
        wrapped.append(pltpu.roll(jnp.where(sub == SUBLANES - 1, prev, cur), 1, 0))
        tail_scr[i * SUBLANES:(i + 1) * SUBLANES, :] = cur

    def x_at(j):
        return xs[j] if j >= 0 else wrapped[j + n_hist]

    cw = cw_ref[...]
    w = [cw[i:i + 1, :] for i in range(CONV_W)]
    cb = cb_ref[...]
    xc = jnp.concatenate(
        [cb + w[3] * xs[j] + w[2] * x_at(j - 1) + w[1] * x_at(j - 2) + w[0] * x_at(j - 3)
         for j in range(n_slab)], axis=0)

    xcb = xc.astype(BF16)
    n_g = d // MXU_TILE
    ra = jnp.concatenate(
        [jnp.dot(xcb[:, g * MXU_TILE:(g + 1) * MXU_TILE], wa_ref[g], preferred_element_type=F32)
         for g in range(n_g)], axis=-1)
    ri = jnp.concatenate(
        [jnp.dot(xcb[:, g * MXU_TILE:(g + 1) * MXU_TILE], wi_ref[g], preferred_element_type=F32)
         for g in range(n_g)], axis=-1)
    r = jax.nn.sigmoid(ra + ba_ref[...])
    gi = jax.nn.sigmoid(ri + bi_ref[...])
    z = -lam_ref[...]
    softplus = jnp.maximum(z, 0.0) + jnp.log1p(jnp.exp(-jnp.abs(z)))
    log_a = (-LRU_C) * r * softplus
    a = jnp.exp(log_a)
    mult = jnp.sqrt(jnp.maximum(1.0 - a * a, 0.0))
    u = xc * gi * mult

    h_loc, p_loc = [slab(u, 0)], [slab(a, 0)]
    for j in range(1, n_slab):
        aj = slab(a, j)
        h_loc.append(aj * h_loc[-1] + slab(u, j))
        p_loc.append(aj * p_loc[-1])
    pa, hb = p_loc[-1], h_loc[-1]
    step = 1
    while step < SUBLANES:
        valid = sub >= step
        hb = jnp.where(valid, pa * pltpu.roll(hb, step, 0) + hb, hb)
        pa = jnp.where(valid, pa * pltpu.roll(pa, step, 0), pa)
        step *= 2
    carry = h_scr[...]
    end = pa * carry + hb
    start = jnp.where(sub == 0, carry, pltpu.roll(end, 1, 0))
    h_scr[...] = end[SUBLANES - 1:SUBLANES, :]
    h = jnp.concatenate([h_loc[j] + p_loc[j] * start for j in range(n_slab)], axis=0)

    ys = jnp.concatenate(permuted(yr_ref, y_scr), axis=0)
    gs = jnp.concatenate(permuted(g_ref, g_scr), axis=0)
    ya = (h * jax.nn.gelu(ys)).astype(BF16)
    out = gs * jnp.dot(ya, p_ref[...], preferred_element_type=F32)
    for j in range(n_slab):
        for c in range(n_lane):
            o_scr[c, pl.ds(j, SUBLANES, stride=pitch), :] = slab(out, j)[:, c * LANES:(c + 1) * LANES]
    o_ref[...] = jnp.concatenate(
        [jnp.concatenate([o_scr[c, s * pitch:s * pitch + n_slab, :] for s in range(SUBLANES)], axis=0)
         for c in range(n_lane)], axis=-1).astype(BF16)


def _rnn(proj, cw, cb, wa, ba, wi, bi, lam, p_rnn, bsz, seq, tt):
    n = proj.shape[0]
    d = D_MODEL
    per_b = seq // tt
    row = lambda b, t: b * per_b + t
    vec = pl.BlockSpec((1, d), lambda b, t: (0, 0))
    gate_w = pl.BlockSpec(wa.shape, lambda b, t: (0, 0, 0))
    tile = pltpu.VMEM((d // LANES, SUBLANES * _stream_pitch(tt // SUBLANES), LANES), F32)
    return pl.pallas_call(
        _rnn_kernel,
        grid=(bsz, per_b),
        in_specs=[pl.BlockSpec((tt, d), lambda b, t: (row(b, t), COL_X)),
                  pl.BlockSpec((tt, d), lambda b, t: (row(b, t), COL_Y)),
                  pl.BlockSpec((tt, d), lambda b, t: (row(b, t), COL_GR)),
                  pl.BlockSpec((CONV_W, d), lambda b, t: (0, 0)),
                  vec, gate_w, vec, gate_w, vec, vec,
                  pl.BlockSpec((d, d), lambda b, t: (0, 0))],
        out_specs=pl.BlockSpec((tt, d), lambda b, t: (row(b, t), 0)),
        out_shape=jax.ShapeDtypeStruct((n, d), BF16),
        scratch_shapes=[tile, tile, tile, tile,
                        pltpu.VMEM(((CONV_W - 1) * SUBLANES, d), F32), pltpu.VMEM((1, d), F32)],
        compiler_params=_params(("parallel", "arbitrary")),
        name="rnn",
    )(proj, proj, proj, cw, cb, wa, ba, wi, bi, lam, p_rnn)


def _attn_kernel(q_ref, k_ref, v_ref, lq1_ref, lk1_ref, lq2_ref, lk2_ref, sg_ref, o_ref, *, tq):
    seq = k_ref.shape[0]
    lam = (jnp.exp(jnp.sum(lq1_ref[...] * lk1_ref[...], axis=-1, keepdims=True))
           - jnp.exp(jnp.sum(lq2_ref[...] * lk2_ref[...], axis=-1, keepdims=True)) + LAM_INIT)
    nt = (((1,), (1,)), ((), ()))
    scale = DIFF_DH ** -0.5
    lane = lax.broadcasted_iota(jnp.int32, (tq, HEAD_W), 1)
    diag = (lax.broadcasted_iota(jnp.int32, (tq, tq), 1) // CHUNK
            <= lax.broadcasted_iota(jnp.int32, (tq, tq), 0) // CHUNK)

    diag2 = jnp.concatenate([diag, diag], axis=0)

    for qi in reversed(range(seq // tq)):
        kv = (qi + 1) * tq
        q = q_ref[qi * tq:(qi + 1) * tq, :] * scale
        zero = jnp.zeros_like(q)
        qq = jnp.concatenate([jnp.where(lane < DIFF_DH, q, zero), jnp.where(lane >= DIFF_DH, q, zero)], axis=0)
        s = lax.dot_general(qq, k_ref[0:kv, :], nt, preferred_element_type=F32)
        s_diag = jnp.where(diag2, s[:, kv - tq:], -jnp.inf)
        s = s_diag if kv == tq else jnp.concatenate([s[:, :kv - tq], s_diag], axis=1)
        e = jnp.exp(s - jnp.max(s, axis=-1, keepdims=True))
        pv = jnp.dot(e.astype(BF16), v_ref[0:kv, :], preferred_element_type=F32)
        pv = pv / jnp.sum(e, axis=-1, keepdims=True)
        o = pv[:tq] - lam * pv[tq:]
        o = o * lax.rsqrt(jnp.mean(o * o, axis=-1, keepdims=True) + RMS_EPS) * sg_ref[...]
        o_ref[qi * tq:(qi + 1) * tq, :] = (o * (1.0 - LAM_INIT)).astype(BF16)


def _attn(proj, lq1, lk1, lq2, lk2, sg, bsz, seq, tq):
    n = proj.shape[0]
    hb = D_MODEL // HEAD_W
    lam_spec = pl.BlockSpec((1, DIFF_DH), lambda b, h: (0, 0))
    return pl.pallas_call(
        functools.partial(_attn_kernel, tq=tq),
        grid=(bsz, N_HEADS),
        in_specs=[pl.BlockSpec((seq, HEAD_W), lambda b, h: (b, COL_Q * hb + h)),
                  pl.BlockSpec((seq, HEAD_W), lambda b, h: (b, COL_K * hb + h)),
                  pl.BlockSpec((seq, HEAD_W), lambda b, h: (b, COL_V * hb + h)),
                  lam_spec, lam_spec, lam_spec, lam_spec,
                  pl.BlockSpec((1, HEAD_W), lambda b, h: (0, 0))],
        out_specs=pl.BlockSpec((seq, HEAD_W), lambda b, h: (b, h)),
        out_shape=jax.ShapeDtypeStruct((n, D_MODEL), BF16),
        compiler_params=_params(("parallel", "parallel")),
        name="attn",
    )(proj, proj, proj, lq1, lk1, lq2, lk2, sg)


def _postmix_kernel(yb_ref, ma_ref, ga_ref, x_ref, g1_ref, sh_ref, sc_ref, lg_ref, lb_ref,
                    pa_ref, wo_ref, wr_ref, x1_ref, u_ref, lt_ref):
    tm = x_ref.shape[0]

    def project(rows):
        branch_b = jnp.dot(yb_ref[rows, :], pa_ref[...], preferred_element_type=F32)
        merged = ma_ref[rows, :].astype(F32) + ga_ref[rows, :].astype(F32) * branch_b
        return jnp.dot(merged.astype(BF16), wo_ref[...], preferred_element_type=F32)

    def normalize(rows, mix):
        x1 = _ln0(ALPHA * x_ref[rows, :] + g1_ref[0] * mix) * lg_ref[...] + lb_ref[...]
        x1_ref[rows, :] = x1
        u = _ln0(x1) * (1.0 + sc_ref[0]) + sh_ref[0]
        u_ref[rows, :] = u.astype(BF16)
        u_hi = u.astype(BF16)
        u_lo = (u - u_hi.astype(F32)).astype(BF16)
        a = jnp.dot(u_hi, wr_ref[...], preferred_element_type=F32)
        b = jnp.dot(u_lo, wr_ref[...], preferred_element_type=F32)
        lt_ref[rows, :] = a + pltpu.roll(a, N_EXPERTS, 1) + b

    chunks = [slice(r0, r0 + POSTMIX_CHUNK) for r0 in range(0, tm, POSTMIX_CHUNK)]
    mix = project(chunks[0])
    for prev, rows in zip(chunks[:-1], chunks[1:]):
        nxt = project(rows)
        normalize(prev, mix)
        mix = nxt
    normalize(chunks[-1], mix)


def _postmix(yb, ma, proj, x2, g1, sh2, sc2, ln_g, ln_b, p_attn, w_out, wr_hilo, seq, tm):
    n, d = x2.shape
    per_b = seq // tm
    row = pl.BlockSpec((tm, d), lambda i: (i, 0))
    mod = pl.BlockSpec((1, 1, d), lambda i: (i // per_b, 0, 0))
    vec = pl.BlockSpec((1, d), lambda i: (0, 0))
    mat = pl.BlockSpec((d, d), lambda i: (0, 0))
    return pl.pallas_call(
        _postmix_kernel,
        grid=(n // tm,),
        in_specs=[row, row, pl.BlockSpec((tm, d), lambda i: (i, COL_GA)), row,
                  mod, mod, mod, vec, vec, mat, mat,
                  pl.BlockSpec((d, 2 * N_EXPERTS), lambda i: (0, 0))],
        out_specs=[row, row, pl.BlockSpec((tm, 2 * N_EXPERTS), lambda i: (i, 0))],
        out_shape=[jax.ShapeDtypeStruct((n, d), F32),
                   jax.ShapeDtypeStruct((n, d), BF16),
                   jax.ShapeDtypeStruct((n, 2 * N_EXPERTS), F32)],
        compiler_params=_params(("parallel",)),
        name="postmix",
    )(yb, ma, proj, x2, g1, sh2, sc2, ln_g, ln_b, p_attn, w_out, wr_hilo)


def _route_kernel(lt_ref, bias_ref, gate_ref):
    tn = lt_ref.shape[0]
    shape3 = (N_GROUPS, GROUP_SIZE, tn)
    neg = -jnp.inf
    s = jax.nn.sigmoid(lt_ref[...].T[:N_EXPERTS, :])
    sel = (s + bias_ref[...]).reshape(shape3)
    s = s.reshape(shape3)
    j_idx = lax.broadcasted_iota(jnp.int32, shape3, 1).astype(F32)
    g_idx = lax.broadcasted_iota(jnp.int32, shape3, 0).astype(F32)
    e_idx = g_idx * GROUP_SIZE + j_idx

    m1 = jnp.max(sel, axis=1, keepdims=True)
    first = jnp.min(jnp.where(sel == m1, j_idx, float(GROUP_SIZE)), axis=1, keepdims=True)
    m2 = jnp.max(jnp.where(j_idx == first, neg, sel), axis=1, keepdims=True)
    gscore = m1 + m2

    gi = lax.broadcasted_iota(jnp.int32, gscore.shape, 0).astype(F32)
    gkeep = jnp.zeros(gscore.shape, jnp.bool_)
    for _ in range(TOPK_GROUPS):
        m = jnp.max(gscore, axis=0, keepdims=True)
        pick = jnp.min(jnp.where(gscore == m, gi, float(N_GROUPS)), axis=0, keepdims=True)
        hit = gi == pick
        gkeep = jnp.logical_or(gkeep, hit)
        gscore = jnp.where(hit, neg, gscore)
    sel = jnp.where(gkeep, sel, neg)

    picked = jnp.zeros(shape3, F32)
    for _ in range(TOP_K):
        m = jnp.max(jnp.max(sel, axis=1, keepdims=True), axis=0, keepdims=True)
        cand = jnp.where(sel == m, e_idx, float(N_EXPERTS))
        pick = jnp.min(jnp.min(cand, axis=1, keepdims=True), axis=0, keepdims=True)
        hit = e_idx == pick
        picked = jnp.where(hit, s, picked)
        sel = jnp.where(hit, neg, sel)
    total = jnp.sum(jnp.sum(picked, axis=1, keepdims=True), axis=0, keepdims=True)
    gate = picked / total * ROUTED_SCALE
    gate = gate.reshape(N_EXPERTS, tn)
    shared = (lax.broadcasted_iota(jnp.int32, gate.shape, 0) == 0).astype(F32)
    gate_ref[...] = jnp.concatenate([gate, shared], axis=0).T


def _route(logits, bias, tn):
    n, w = logits.shape
    e = bias.shape[0]
    return pl.pallas_call(
        _route_kernel,
        grid=(n // tn,),
        in_specs=[pl.BlockSpec((tn, w), lambda i: (i, 0)),
                  pl.BlockSpec((e, 1), lambda i: (0, 0))],
        out_specs=pl.BlockSpec((tn, w), lambda i: (i, 0)),
        out_shape=jax.ShapeDtypeStruct((n, w), F32),
        compiler_params=_params(("parallel",)),
        name="route",
    )(logits, bias.reshape(e, 1))


def _moe_kernel(*refs, eg):
    u_ref, gate_ref, x1_ref, g2_ref = refs[:4]
    wg_refs, wu_refs, wd_refs = refs[4:4 + eg], refs[4 + eg:4 + 2 * eg], refs[4 + 2 * eg:4 + 3 * eg]
    sg_ref, su_ref, sd_ref, lg_ref, lb_ref, o_ref, acc_scr = refs[4 + 3 * eg:]
    j = pl.program_id(1)
    last = j == pl.num_programs(1) - 1

    @pl.when(j == 0)
    def _():
        acc_scr[...] = jnp.zeros_like(acc_scr)

    u = u_ref[...]
    gate = gate_ref[...]
    lane = lax.broadcasted_iota(jnp.int32, gate.shape, 1)
    acts, downs = [], []
    for el in range(eg):
        wg, wu, wd = wg_refs[el][0], wu_refs[el][0], wd_refs[el][0]
        if el == eg - 1:
            wg = jnp.where(last, sg_ref[...], wg)
            wu = jnp.where(last, su_ref[...], wu)
            wd = jnp.where(last, sd_ref[...], wd)
        gk = jnp.sum(jnp.where(lane == j * eg + el, gate, 0.0), axis=-1, keepdims=True)
        hg = jnp.dot(u, wg, preferred_element_type=F32)
        hu = jnp.dot(u, wu, preferred_element_type=F32)
        acts.append((_silu(hg) * hu * gk).astype(BF16))
        downs.append(wd)
    acc_scr[...] += jnp.dot(jnp.concatenate(acts, axis=-1), jnp.concatenate(downs, axis=0),
                            preferred_element_type=F32)

    @pl.when(last)
    def _():
        z = ALPHA * x1_ref[...] + g2_ref[0] * acc_scr[...]
        o_ref[...] = _ln0(z) * lg_ref[...] + lb_ref[...]


def _moe(u, gate, x1, g2, wg, wu, wd, sg, su, sd, ln_g, ln_b, seq, tm, eg):
    n, d = x1.shape
    n_e, _, f = wg.shape
    assert (n_e + 1) % eg == 0 and n_e < gate.shape[1] and sg.shape == (d, f)
    per_b = seq // tm
    row = pl.BlockSpec((tm, d), lambda i, j: (i, 0))
    vec = pl.BlockSpec((1, d), lambda i, j: (0, 0))
    const = lambda a: pl.BlockSpec(a.shape, lambda i, j: (0, 0))

    def expert(shape, el):
        return pl.BlockSpec((1,) + shape, lambda i, j: (jnp.minimum(j * eg + el, n_e - 1), 0, 0))

    slots = range(eg)
    return pl.pallas_call(
        functools.partial(_moe_kernel, eg=eg),
        grid=(n // tm, (n_e + 1) // eg),
        in_specs=[row, pl.BlockSpec((tm, gate.shape[1]), lambda i, j: (i, 0)), row,
                  pl.BlockSpec((1, 1, d), lambda i, j: (i // per_b, 0, 0)),
                  *[expert((d, f), el) for el in slots], *[expert((d, f), el) for el in slots],
                  *[expert((f, d), el) for el in slots],
                  const(sg), const(su), const(sd), vec, vec],
        out_specs=row,
        out_shape=jax.ShapeDtypeStruct((n, d), F32),
        scratch_shapes=[pltpu.VMEM((tm, d), F32)],
        compiler_params=_params(("parallel", "arbitrary")),
        name="moe",
    )(u, gate, x1, g2, *[wg] * eg, *[wu] * eg, *[wd] * eg, sg, su, sd, ln_g, ln_b)


def _block_diag_tiles(w):
    nb, bw, _ = w.shape
    per = MXU_TILE // bw
    w = w.reshape(nb // per, per, bw, bw)
    eye = jnp.eye(per, dtype=w.dtype)
    tiles = jnp.einsum("gpij,pq->gpiqj", w, eye)
    return tiles.reshape(nb // per, MXU_TILE, MXU_TILE)


def kernel(x, c, w_ada, b_ada, w_in, conv_w, conv_b, lru_w_a, lru_b_a, lru_w_i, lru_b_i, lru_lambda,
           lam_q1, lam_k1, lam_q2, lam_k2, subln_g, p_rnn, p_attn, w_out, ln1_g, ln1_b,
           w_router, router_bias, we_gate, we_up, we_down, ws_gate, ws_up, ws_down, ln2_g, ln2_b):
    bsz, seq, d = x.shape
    n = bsz * seq
    x2 = x.reshape(n, d)
    l = 0

    ada = _ada(c, w_ada[l], b_ada[l])
    sh1, sc1, g1, sh2, sc2, g2 = [ada[:, i * d:(i + 1) * d].reshape(bsz, 1, d) for i in range(6)]

    proj = _inproj(x2, sh1, sc1, w_in[l], seq, tm=TILES["inproj_rows"])

    ma = _rnn(proj, conv_w[l], conv_b[l].reshape(1, d),
              _block_diag_tiles(lru_w_a[l]).astype(BF16), lru_b_a[l].reshape(1, d),
              _block_diag_tiles(lru_w_i[l]).astype(BF16), lru_b_i[l].reshape(1, d),
              lru_lambda[l].reshape(1, d), p_rnn[l].astype(BF16), bsz, seq, tt=TILES["rnn_rows"])

    yb = _attn(proj, lam_q1[l].reshape(1, -1), lam_k1[l].reshape(1, -1),
               lam_q2[l].reshape(1, -1), lam_k2[l].reshape(1, -1),
               subln_g[l].reshape(1, -1), bsz, seq, tq=TILES["attn_q_rows"])

    wr_hi = w_router[l].astype(BF16)
    wr_lo = (w_router[l] - wr_hi.astype(F32)).astype(BF16)
    x1, u, logits = _postmix(yb, ma, proj, x2, g1, sh2, sc2, ln1_g[l].reshape(1, d),
                             ln1_b[l].reshape(1, d), p_attn[l].astype(BF16),
                             w_out[l].astype(BF16), jnp.concatenate([wr_hi, wr_lo], axis=1), seq,
                             tm=TILES["postmix_rows"])

    gate = _route(logits, router_bias[l], tn=TILES["route_tokens"])

    out = _moe(u, gate, x1, g2, we_gate[l].astype(BF16), we_up[l].astype(BF16), we_down[l].astype(BF16),
               ws_gate[l].astype(BF16), ws_up[l].astype(BF16), ws_down[l].astype(BF16),
               ln2_g[l].reshape(1, d), ln2_b[l].reshape(1, d),
               seq, tm=TILES["moe_rows"], eg=TILES["moe_experts"])
    return out.reshape(bsz, seq, d)
```

```python
import functools
import math

import jax
import jax.numpy as jnp
from jax import lax
from jax.experimental import pallas as pl
from jax.experimental.pallas import tpu as pltpu

F32 = jnp.float32
BF16 = jnp.bfloat16

D_MODEL = 1024
CHUNK = 64
RNN_BLOCKS = 16
CONV_W = 4
LRU_C = 8.0
N_HEADS = 8
DIFF_DH = 64
HEAD_W = 2 * DIFF_DH
N_EXPERTS = 64
TOP_K = 8
N_GROUPS = 8
GROUP_SIZE = N_EXPERTS // N_GROUPS
TOPK_GROUPS = 4
D_EXPERT = 256
ROUTED_SCALE = 2.5
DEPTH = 1
ALPHA = (2.0 * DEPTH) ** 0.25
LN_EPS = 1e-5
RMS_EPS = 1e-5
LAM_INIT = 0.8 - 0.6 * math.exp(-0.3 * 0)

COL_X, COL_Y, COL_Q, COL_K, COL_V, COL_GR, COL_GA = range(7)
N_COLS = 7

V7X_VMEM_LIMIT = 56 * 1024 * 1024
MXU_TILE = 256
SUBLANES = 8
LANES = 128
POSTMIX_CHUNK = 512

TILES = dict(
    inproj_rows=512,
    rnn_rows=512,
    attn_q_rows=256,
    postmix_rows=1024,
    route_tokens=1024,
    moe_rows=1024,
    moe_experts=5,
)


def _ln0(x):
    mu = jnp.mean(x, axis=-1, keepdims=True)
    xc = x - mu
    var = jnp.mean(xc * xc, axis=-1, keepdims=True)
    return xc * lax.rsqrt(var + LN_EPS)


def _silu(x):
    return x * jax.nn.sigmoid(x)


def _params(sem):
    return pltpu.CompilerParams(dimension_semantics=sem, vmem_limit_bytes=V7X_VMEM_LIMIT)


def _ada_kernel(c_ref, w_ref, b_ref, o_ref):
    cond = _silu(c_ref[...])
    o_ref[...] = jnp.dot(cond, w_ref[...], preferred_element_type=F32,
                         precision=lax.Precision.HIGHEST) + b_ref[...]


def _ada(c, w, b):
    bsz, d = c.shape
    n_out = w.shape[1]
    return pl.pallas_call(
        _ada_kernel,
        grid=(n_out // d,),
        in_specs=[pl.BlockSpec((bsz, d), lambda j: (0, 0)),
                  pl.BlockSpec((d, d), lambda j: (0, j)),
                  pl.BlockSpec((1, d), lambda j: (0, j))],
        out_specs=pl.BlockSpec((bsz, d), lambda j: (0, j)),
        out_shape=jax.ShapeDtypeStruct((bsz, n_out), F32),
        compiler_params=_params(("parallel",)),
        name="ada",
    )(c, w, b.reshape(1, n_out))


def _inproj_kernel(x_ref, sh_ref, sc_ref, w_ref, o_ref):
    d = x_ref.shape[1]
    h = (_ln0(x_ref[...]) * (1.0 + sc_ref[0]) + sh_ref[0]).astype(BF16)
    for j in range(N_COLS):
        cols = slice(j * d, (j + 1) * d)
        acc = jnp.dot(h, w_ref[:, cols], preferred_element_type=F32)
        if j >= COL_GR:
            acc = jax.nn.sigmoid(acc)
        o_ref[:, cols] = acc.astype(BF16)


def _inproj(x2, sh, sc, w, seq, tm):
    n, d = x2.shape
    per_b = seq // tm
    return pl.pallas_call(
        _inproj_kernel,
        grid=(n // tm,),
        in_specs=[pl.BlockSpec((tm, d), lambda i: (i, 0)),
                  pl.BlockSpec((1, 1, d), lambda i: (i // per_b, 0, 0)),
                  pl.BlockSpec((1, 1, d), lambda i: (i // per_b, 0, 0)),
                  pl.BlockSpec(w.shape, lambda i: (0, 0))],
        out_specs=pl.BlockSpec((tm, N_COLS * d), lambda i: (i, 0)),
        out_shape=jax.ShapeDtypeStruct((n, N_COLS * d), BF16),
        compiler_params=_params(("parallel",)),
        name="inproj",
    )(x2, sh, sc, w)


def _stream_pitch(n_slab):
    assert n_slab % SUBLANES == 0
    return n_slab if (n_slab // SUBLANES) % 2 else n_slab + SUBLANES


def _rnn_kernel(xr_ref, yr_ref, g_ref, cw_ref, cb_ref, wa_ref, ba_ref, wi_ref, bi_ref,
                lam_ref, p_ref, o_ref, x_scr, y_scr, g_scr, o_scr, tail_scr, h_scr):
    t = pl.program_id(1)
    tt, d = xr_ref.shape
    n_slab = tt // SUBLANES
    n_hist = CONV_W - 1

    @pl.when(t == 0)
    def _():
        tail_scr[...] = jnp.zeros_like(tail_scr)
        h_scr[...] = jnp.zeros_like(h_scr)

    def slab(v, j):
        return v[j * SUBLANES:(j + 1) * SUBLANES, :]

    n_lane = d // LANES
    pitch = _stream_pitch(n_slab)

    def permuted(src_ref, scr):
        x = src_ref[...].astype(F32)
        for c in range(n_lane):
            for s in range(SUBLANES):
                scr[c, s * pitch:s * pitch + n_slab, :] = x[s * n_slab:(s + 1) * n_slab, c * LANES:(c + 1) * LANES]
        return [jnp.concatenate([scr[c, pl.ds(j, SUBLANES, stride=pitch), :] for c in range(n_lane)],
                                axis=-1) for j in range(n_slab)]

    sub = lax.broadcasted_iota(jnp.int32, (SUBLANES, d), 0)

    xs = permuted(xr_ref, x_scr)
    wrapped = []
    for i in range(n_hist):
        cur = xs[n_slab - n_hist + i]
        prev = slab(tail_scr, i)
        wrapped.append(pltpu.roll(jnp.where(sub == SUBLANES - 1, prev, cur), 1, 0))
        tail_scr[i * SUBLANES:(i + 1) * SUBLANES, :] = cur

    def x_at(j):
        return xs[j] if j >= 0 else wrapped[j + n_hist]

    cw = cw_ref[...]
    w = [cw[i:i + 1, :] for i in range(CONV_W)]
    cb = cb_ref[...]
    xc = jnp.concatenate(
        [cb + w[3] * xs[j] + w[2] * x_at(j - 1) + w[1] * x_at(j - 2) + w[0] * x_at(j - 3)
         for j in range(n_slab)], axis=0)

    xcb = xc.astype(BF16)
    n_g = d // MXU_TILE
    ra = jnp.concatenate(
        [jnp.dot(xcb[:, g * MXU_TILE:(g + 1) * MXU_TILE], wa_ref[g], preferred_element_type=F32)
         for g in range(n_g)], axis=-1)
    ri = jnp.concatenate(
        [jnp.dot(xcb[:, g * MXU_TILE:(g + 1) * MXU_TILE], wi_ref[g], preferred_element_type=F32)
         for g in range(n_g)], axis=-1)
    r = jax.nn.sigmoid(ra + ba_ref[...])
    gi = jax.nn.sigmoid(ri + bi_ref[...])
    z = -lam_ref[...]
    softplus = jnp.maximum(z, 0.0) + jnp.log1p(jnp.exp(-jnp.abs(z)))
    log_a = (-LRU_C) * r * softplus
    a = jnp.exp(log_a)
    mult = jnp.sqrt(jnp.maximum(1.0 - a * a, 0.0))
    u = xc * gi * mult

    h_loc, p_loc = [slab(u, 0)], [slab(a, 0)]
    for j in range(1, n_slab):
        aj = slab(a, j)
        h_loc.append(aj * h_loc[-1] + slab(u, j))
        p_loc.append(aj * p_loc[-1])
    pa, hb = p_loc[-1], h_loc[-1]
    step = 1
    while step < SUBLANES:
        valid = sub >= step
        hb = jnp.where(valid, pa * pltpu.roll(hb, step, 0) + hb, hb)
        pa = jnp.where(valid, pa * pltpu.roll(pa, step, 0), pa)
        step *= 2
    carry = h_scr[...]
    end = pa * carry + hb
    start = jnp.where(sub == 0, carry, pltpu.roll(end, 1, 0))
    h_scr[...] = end[SUBLANES - 1:SUBLANES, :]
    h = jnp.concatenate([h_loc[j] + p_loc[j] * start for j in range(n_slab)], axis=0)

    ys = jnp.concatenate(permuted(yr_ref, y_scr), axis=0)
    gs = jnp.concatenate(permuted(g_ref, g_scr), axis=0)
    ya = (h * jax.nn.gelu(ys)).astype(BF16)
    out = gs * jnp.dot(ya, p_ref[...], preferred_element_type=F32)
    for j in range(n_slab):
        for c in range(n_lane):
            o_scr[c, pl.ds(j, SUBLANES, stride=pitch), :] = slab(out, j)[:, c * LANES:(c + 1) * LANES]
    o_ref[...] = jnp.concatenate(
        [jnp.concatenate([o_scr[c, s * pitch:s * pitch + n_slab, :] for s in range(SUBLANES)], axis=0)
         for c in range(n_lane)], axis=-1).astype(BF16)


def _rnn(proj, cw, cb, wa, ba, wi, bi, lam, p_rnn, bsz, seq, tt):
    n = proj.shape[0]
    d = D_MODEL
    per_b = seq // tt
    row = lambda b, t: b * per_b + t
    vec = pl.BlockSpec((1, d), lambda b, t: (0, 0))
    gate_w = pl.BlockSpec(wa.shape, lambda b, t: (0, 0, 0))
    tile = pltpu.VMEM((d // LANES, SUBLANES * _stream_pitch(tt // SUBLANES), LANES), F32)
    return pl.pallas_call(
        _rnn_kernel,
        grid=(bsz, per_b),
        in_specs=[pl.BlockSpec((tt, d), lambda b, t: (row(b, t), COL_X)),
                  pl.BlockSpec((tt, d), lambda b, t: (row(b, t), COL_Y)),
                  pl.BlockSpec((tt, d), lambda b, t: (row(b, t), COL_GR)),
                  pl.BlockSpec((CONV_W, d), lambda b, t: (0, 0)),
                  vec, gate_w, vec, gate_w, vec, vec,
                  pl.BlockSpec((d, d), lambda b, t: (0, 0))],
        out_specs=pl.BlockSpec((tt, d), lambda b, t: (row(b, t), 0)),
        out_shape=jax.ShapeDtypeStruct((n, d), BF16),
        scratch_shapes=[tile, tile, tile, tile,
                        pltpu.VMEM(((CONV_W - 1) * SUBLANES, d), F32), pltpu.VMEM((1, d), F32)],
        compiler_params=_params(("parallel", "arbitrary")),
        name="rnn",
    )(proj, proj, proj, cw, cb, wa, ba, wi, bi, lam, p_rnn)


def _attn_kernel(q_ref, k_ref, v_ref, lq1_ref, lk1_ref, lq2_ref, lk2_ref, sg_ref, o_ref, *, tq):
    seq = k_ref.shape[0]
    lam = (jnp.exp(jnp.sum(lq1_ref[...] * lk1_ref[...], axis=-1, keepdims=True))
           - jnp.exp(jnp.sum(lq2_ref[...] * lk2_ref[...], axis=-1, keepdims=True)) + LAM_INIT)
    nt = (((1,), (1,)), ((), ()))
    scale = DIFF_DH ** -0.5
    lane = lax.broadcasted_iota(jnp.int32, (tq, HEAD_W), 1)
    diag = (lax.broadcasted_iota(jnp.int32, (tq, tq), 1) // CHUNK
            <= lax.broadcasted_iota(jnp.int32, (tq, tq), 0) // CHUNK)

    diag2 = jnp.concatenate([diag, diag], axis=0)

    for qi in reversed(range(seq // tq)):
        kv = (qi + 1) * tq
        q = q_ref[qi * tq:(qi + 1) * tq, :] * scale
        zero = jnp.zeros_like(q)
        qq = jnp.concatenate([jnp.where(lane < DIFF_DH, q, zero), jnp.where(lane >= DIFF_DH, q, zero)], axis=0)
        s = lax.dot_general(qq, k_ref[0:kv, :], nt, preferred_element_type=F32)
        s_diag = jnp.where(diag2, s[:, kv - tq:], -jnp.inf)
        s = s_diag if kv == tq else jnp.concatenate([s[:, :kv - tq], s_diag], axis=1)
        e = jnp.exp(s - jnp.max(s, axis=-1, keepdims=True))
        pv = jnp.dot(e.astype(BF16), v_ref[0:kv, :], preferred_element_type=F32)
        pv = pv / jnp.sum(e, axis=-1, keepdims=True)
        o = pv[:tq] - lam * pv[tq:]
        o = o * lax.rsqrt(jnp.mean(o * o, axis=-1, keepdims=True) + RMS_EPS) * sg_ref[...]
        o_ref[qi * tq:(qi + 1) * tq, :] = (o * (1.0 - LAM_INIT)).astype(BF16)


def _attn(proj, lq1, lk1, lq2, lk2, sg, bsz, seq, tq):
    n = proj.shape[0]
    hb = D_MODEL // HEAD_W
    lam_spec = pl.BlockSpec((1, DIFF_DH), lambda b, h: (0, 0))
    return pl.pallas_call(
        functools.partial(_attn_kernel, tq=tq),
        grid=(bsz, N_HEADS),
        in_specs=[pl.BlockSpec((seq, HEAD_W), lambda b, h: (b, COL_Q * hb + h)),
                  pl.BlockSpec((seq, HEAD_W), lambda b, h: (b, COL_K * hb + h)),
                  pl.BlockSpec((seq, HEAD_W), lambda b, h: (b, COL_V * hb + h)),
                  lam_spec, lam_spec, lam_spec, lam_spec,
                  pl.BlockSpec((1, HEAD_W), lambda b, h: (0, 0))],
        out_specs=pl.BlockSpec((seq, HEAD_W), lambda b, h: (b, h)),
        out_shape=jax.ShapeDtypeStruct((n, D_MODEL), BF16),
        compiler_params=_params(("parallel", "parallel")),
        name="attn",
    )(proj, proj, proj, lq1, lk1, lq2, lk2, sg)


def _postmix_kernel(yb_ref, ma_ref, ga_ref, x_ref, g1_ref, sh_ref, sc_ref, lg_ref, lb_ref,
                    pa_ref, wo_ref, wr_ref, x1_ref, u_ref, lt_ref):
    tm = x_ref.shape[0]

    def project(rows):
        branch_b = jnp.dot(yb_ref[rows, :], pa_ref[...], preferred_element_type=F32)
        merged = ma_ref[rows, :].astype(F32) + ga_ref[rows, :].astype(F32) * branch_b
        return jnp.dot(merged.astype(BF16), wo_ref[...], preferred_element_type=F32)

    def normalize(rows, mix):
        x1 = _ln0(ALPHA * x_ref[rows, :] + g1_ref[0] * mix) * lg_ref[...] + lb_ref[...]
        x1_ref[rows, :] = x1
        u = _ln0(x1) * (1.0 + sc_ref[0]) + sh_ref[0]
        u_ref[rows, :] = u.astype(BF16)
        u_hi = u.astype(BF16)
        u_lo = (u - u_hi.astype(F32)).astype(BF16)
        a = jnp.dot(u_hi, wr_ref[...], preferred_element_type=F32)
        b = jnp.dot(u_lo, wr_ref[...], preferred_element_type=F32)
        lt_ref[rows, :] = a + pltpu.roll(a, N_EXPERTS, 1) + b

    chunks = [slice(r0, r0 + POSTMIX_CHUNK) for r0 in range(0, tm, POSTMIX_CHUNK)]
    mix = project(chunks[0])
    for prev, rows in zip(chunks[:-1], chunks[1:]):
        nxt = project(rows)
        normalize(prev, mix)
        mix = nxt
    normalize(chunks[-1], mix)


def _postmix(yb, ma, proj, x2, g1, sh2, sc2, ln_g, ln_b, p_attn, w_out, wr_hilo, seq, tm):
    n, d = x2.shape
    per_b = seq // tm
    row = pl.BlockSpec((tm, d), lambda i: (i, 0))
    mod = pl.BlockSpec((1, 1, d), lambda i: (i // per_b, 0, 0))
    vec = pl.BlockSpec((1, d), lambda i: (0, 0))
    mat = pl.BlockSpec((d, d), lambda i: (0, 0))
    return pl.pallas_call(
        _postmix_kernel,
        grid=(n // tm,),
        in_specs=[row, row, pl.BlockSpec((tm, d), lambda i: (i, COL_GA)), row,
                  mod, mod, mod, vec, vec, mat, mat,
                  pl.BlockSpec((d, 2 * N_EXPERTS), lambda i: (0, 0))],
        out_specs=[row, row, pl.BlockSpec((tm, 2 * N_EXPERTS), lambda i: (i, 0))],
        out_shape=[jax.ShapeDtypeStruct((n, d), F32),
                   jax.ShapeDtypeStruct((n, d), BF16),
                   jax.ShapeDtypeStruct((n, 2 * N_EXPERTS), F32)],
        compiler_params=_params(("parallel",)),
        name="postmix",
    )(yb, ma, proj, x2, g1, sh2, sc2, ln_g, ln_b, p_attn, w_out, wr_hilo)


def _route_kernel(lt_ref, bias_ref, gate_ref):
    tn = lt_ref.shape[0]
    shape3 = (N_GROUPS, GROUP_SIZE, tn)
    neg = -jnp.inf
    s = jax.nn.sigmoid(lt_ref[...].T[:N_EXPERTS, :])
    sel = (s + bias_ref[...]).reshape(shape3)
    s = s.reshape(shape3)
    j_idx = lax.broadcasted_iota(jnp.int32, shape3, 1).astype(F32)
    g_idx = lax.broadcasted_iota(jnp.int32, shape3, 0).astype(F32)
    e_idx = g_idx * GROUP_SIZE + j_idx

    m1 = jnp.max(sel, axis=1, keepdims=True)
    first = jnp.min(jnp.where(sel == m1, j_idx, float(GROUP_SIZE)), axis=1, keepdims=True)
    m2 = jnp.max(jnp.where(j_idx == first, neg, sel), axis=1, keepdims=True)
    gscore = m1 + m2

    gi = lax.broadcasted_iota(jnp.int32, gscore.shape, 0).astype(F32)
    gkeep = jnp.zeros(gscore.shape, jnp.bool_)
    for _ in range(TOPK_GROUPS):
        m = jnp.max(gscore, axis=0, keepdims=True)
        pick = jnp.min(jnp.where(gscore == m, gi, float(N_GROUPS)), axis=0, keepdims=True)
        hit = gi == pick
        gkeep = jnp.logical_or(gkeep, hit)
        gscore = jnp.where(hit, neg, gscore)
    sel = jnp.where(gkeep, sel, neg)

    picked = jnp.zeros(shape3, F32)
    for _ in range(TOP_K):
        m = jnp.max(jnp.max(sel, axis=1, keepdims=True), axis=0, keepdims=True)
        cand = jnp.where(sel == m, e_idx, float(N_EXPERTS))
        pick = jnp.min(jnp.min(cand, axis=1, keepdims=True), axis=0, keepdims=True)
        hit = e_idx == pick
        picked = jnp.where(hit, s, picked)
        sel = jnp.where(hit, neg, sel)
    total = jnp.sum(jnp.sum(picked, axis=1, keepdims=True), axis=0, keepdims=True)
    gate = picked / total * ROUTED_SCALE
    gate = gate.reshape(N_EXPERTS, tn)
    shared = (lax.broadcasted_iota(jnp.int32, gate.shape, 0) == 0).astype(F32)
    gate_ref[...] = jnp.concatenate([gate, shared], axis=0).T


def _route(logits, bias, tn):
    n, w = logits.shape
    e = bias.shape[0]
    return pl.pallas_call(
        _route_kernel,
        grid=(n // tn,),
        in_specs=[pl.BlockSpec((tn, w), lambda i: (i, 0)),
                  pl.BlockSpec((e, 1), lambda i: (0, 0))],
        out_specs=pl.BlockSpec((tn, w), lambda i: (i, 0)),
        out_shape=jax.ShapeDtypeStruct((n, w), F32),
        compiler_params=_params(("parallel",)),
        name="route",
    )(logits, bias.reshape(e, 1))


def _moe_kernel(*refs, eg):
    u_ref, gate_ref, x1_ref, g2_ref = refs[:4]
    wg_refs, wu_refs, wd_refs = refs[4:4 + eg], refs[4 + eg:4 + 2 * eg], refs[4 + 2 * eg:4 + 3 * eg]
    sg_ref, su_ref, sd_ref, lg_ref, lb_ref, o_ref, acc_scr = refs[4 + 3 * eg:]
    j = pl.program_id(1)
    last = j == pl.num_programs(1) - 1

    @pl.when(j == 0)
    def _():
        acc_scr[...] = jnp.zeros_like(acc_scr)

    u = u_ref[...]
    gate = gate_ref[...]
    lane = lax.broadcasted_iota(jnp.int32, gate.shape, 1)
    acts, downs = [], []
    for el in range(eg):
        wg, wu, wd = wg_refs[el][0], wu_refs[el][0], wd_refs[el][0]
        if el == eg - 1:
            wg = jnp.where(last, sg_ref[...], wg)
            wu = jnp.where(last, su_ref[...], wu)
            wd = jnp.where(last, sd_ref[...], wd)
        gk = jnp.sum(jnp.where(lane == j * eg + el, gate, 0.0), axis=-1, keepdims=True)
        hg = jnp.dot(u, wg, preferred_element_type=F32)
        hu = jnp.dot(u, wu, preferred_element_type=F32)
        acts.append((_silu(hg) * hu * gk).astype(BF16))
        downs.append(wd)
    acc_scr[...] += jnp.dot(jnp.concatenate(acts, axis=-1), jnp.concatenate(downs, axis=0),
                            preferred_element_type=F32)

    @pl.when(last)
    def _():
        z = ALPHA * x1_ref[...] + g2_ref[0] * acc_scr[...]
        o_ref[...] = _ln0(z) * lg_ref[...] + lb_ref[...]


def _moe(u, gate, x1, g2, wg, wu, wd, sg, su, sd, ln_g, ln_b, seq, tm, eg):
    n, d = x1.shape
    n_e, _, f = wg.shape
    assert (n_e + 1) % eg == 0 and n_e < gate.shape[1] and sg.shape == (d, f)
    per_b = seq // tm
    row = pl.BlockSpec((tm, d), lambda i, j: (i, 0))
    vec = pl.BlockSpec((1, d), lambda i, j: (0, 0))
    const = lambda a: pl.BlockSpec(a.shape, lambda i, j: (0, 0))

    def expert(shape, el):
        return pl.BlockSpec((1,) + shape, lambda i, j: (jnp.minimum(j * eg + el, n_e - 1), 0, 0))

    slots = range(eg)
    return pl.pallas_call(
        functools.partial(_moe_kernel, eg=eg),
        grid=(n // tm, (n_e + 1) // eg),
        in_specs=[row, pl.BlockSpec((tm, gate.shape[1]), lambda i, j: (i, 0)), row,
                  pl.BlockSpec((1, 1, d), lambda i, j: (i // per_b, 0, 0)),
                  *[expert((d, f), el) for el in slots], *[expert((d, f), el) for el in slots],
                  *[expert((f, d), el) for el in slots],
                  const(sg), const(su), const(sd), vec, vec],
        out_specs=row,
        out_shape=jax.ShapeDtypeStruct((n, d), F32),
        scratch_shapes=[pltpu.VMEM((tm, d), F32)],
        compiler_params=_params(("parallel", "arbitrary")),
        name="moe",
    )(u, gate, x1, g2, *[wg] * eg, *[wu] * eg, *[wd] * eg, sg, su, sd, ln_g, ln_b)


def _block_diag_tiles(w):
    nb, bw, _ = w.shape
    per = MXU_TILE // bw
    w = w.reshape(nb // per, per, bw, bw)
    eye = jnp.eye(per, dtype=w.dtype)
    tiles = jnp.einsum("gpij,pq->gpiqj", w, eye)
    return tiles.reshape(nb // per, MXU_TILE, MXU_TILE)


def kernel(x, c, w_ada, b_ada, w_in, conv_w, conv_b, lru_w_a, lru_b_a, lru_w_i, lru_b_i, lru_lambda,
           lam_q1, lam_k1, lam_q2, lam_k2, subln_g, p_rnn, p_attn, w_out, ln1_g, ln1_b,
           w_router, router_bias, we_gate, we_up, we_down, ws_gate, ws_up, ws_down, ln2_g, ln2_b):
    bsz, seq, d = x.shape
    n = bsz * seq
    x2 = x.reshape(n, d)
    l = 0

    ada = _ada(c, w_ada[l], b_ada[l])
    sh1, sc1, g1, sh2, sc2, g2 = [ada[:, i * d:(i + 1) * d].reshape(bsz, 1, d) for i in range(6)]

    proj = _inproj(x2, sh1, sc1, w_in[l].astype(BF16), seq, tm=TILES["inproj_rows"])

    ma = _rnn(proj, conv_w[l], conv_b[l].reshape(1, d),
              _block_diag_tiles(lru_w_a[l]).astype(BF16), lru_b_a[l].reshape(1, d),
              _block_diag_tiles(lru_w_i[l]).astype(BF16), lru_b_i[l].reshape(1, d),
              lru_lambda[l].reshape(1, d), p_rnn[l].astype(BF16), bsz, seq, tt=TILES["rnn_rows"])

    yb = _attn(proj, lam_q1[l].reshape(1, -1), lam_k1[l].reshape(1, -1),
               lam_q2[l].reshape(1, -1), lam_k2[l].reshape(1, -1),
               subln_g[l].reshape(1, -1), bsz, seq, tq=TILES["attn_q_rows"])

    wr_hi = w_router[l].astype(BF16)
    wr_lo = (w_router[l] - wr_hi.astype(F32)).astype(BF16)
    x1, u, logits = _postmix(yb, ma, proj, x2, g1, sh2, sc2, ln1_g[l].reshape(1, d),
                             ln1_b[l].reshape(1, d), p_attn[l].astype(BF16),
                             w_out[l].astype(BF16), jnp.concatenate([wr_hi, wr_lo], axis=1), seq,
                             tm=TILES["postmix_rows"])

    gate = _route(logits, router_bias[l], tn=TILES["route_tokens"])

    out = _moe(u, gate, x1, g2, we_gate[l].astype(BF16), we_up[l].astype(BF16), we_down[l].astype(BF16),
               ws_gate[l].astype(BF16), ws_up[l].astype(BF16), ws_down[l].astype(BF16),
               ln2_g[l].reshape(1, d), ln2_b[l].reshape(1, d),
               seq, tm=TILES["moe_rows"], eg=TILES["moe_experts"])
    return out.reshape(bsz, seq, d)
```

```python
import functools
import math

import jax
import jax.numpy as jnp
from jax import lax
from jax.experimental import pallas as pl
from jax.experimental.pallas import tpu as pltpu

F32 = jnp.float32
BF16 = jnp.bfloat16

D_MODEL = 1024
CHUNK = 64
RNN_BLOCKS = 16
CONV_W = 4
LRU_C = 8.0
N_HEADS = 8
DIFF_DH = 64
HEAD_W = 2 * DIFF_DH
N_EXPERTS = 64
TOP_K = 8
N_GROUPS = 8
GROUP_SIZE = N_EXPERTS // N_GROUPS
TOPK_GROUPS = 4
D_EXPERT = 256
ROUTED_SCALE = 2.5
DEPTH = 1
ALPHA = (2.0 * DEPTH) ** 0.25
LN_EPS = 1e-5
RMS_EPS = 1e-5
LAM_INIT = 0.8 - 0.6 * math.exp(-0.3 * 0)

COL_X, COL_Y, COL_Q, COL_K, COL_V, COL_GR, COL_GA = range(7)
N_COLS = 7

V7X_VMEM_LIMIT = 56 * 1024 * 1024
MXU_TILE = 256
SUBLANES = 8
LANES = 128
ADA_STEPS = 8
POSTMIX_CHUNK = 512

TILES = dict(
    inproj_rows=512,
    rnn_rows=512,
    attn_q_rows=256,
    postmix_rows=1024,
    route_tokens=1024,
    moe_rows=1024,
    moe_experts=5,
)


def _ln0(x):
    mu = jnp.mean(x, axis=-1, keepdims=True)
    xc = x - mu
    var = jnp.mean(xc * xc, axis=-1, keepdims=True)
    return xc * lax.rsqrt(var + LN_EPS)


def _silu(x):
    return x * jax.nn.sigmoid(x)


def _params(sem):
    return pltpu.CompilerParams(dimension_semantics=sem, vmem_limit_bytes=V7X_VMEM_LIMIT)


def _ada_kernel(c_ref, w_ref, b_ref, *rest):
    n_cast = len(rest) // 2
    o_ref = rest[n_cast]
    for src, dst in zip(rest[:n_cast], rest[n_cast + 1:]):
        dst[...] = src[...].astype(BF16)
    cond = _silu(c_ref[...])
    o_ref[...] = jnp.dot(cond, w_ref[...], preferred_element_type=F32,
                         precision=lax.Precision.HIGHEST) + b_ref[...]


def _ada(c, w, b, to_bf16):
    bsz, d = c.shape
    n_out = w.shape[1]
    steps = ADA_STEPS
    tn = n_out // steps

    def slab(a):
        assert a.shape[0] % (steps * 2 * SUBLANES) == 0
        return pl.BlockSpec((a.shape[0] // steps, a.shape[1]), lambda j: (j, 0))

    return pl.pallas_call(
        _ada_kernel,
        grid=(steps,),
        in_specs=[pl.BlockSpec((bsz, d), lambda j: (0, 0)),
                  pl.BlockSpec((d, tn), lambda j: (0, j)),
                  pl.BlockSpec((1, tn), lambda j: (0, j)),
                  *[slab(a) for a in to_bf16]],
        out_specs=[pl.BlockSpec((bsz, tn), lambda j: (0, j)), *[slab(a) for a in to_bf16]],
        out_shape=[jax.ShapeDtypeStruct((bsz, n_out), F32),
                   *[jax.ShapeDtypeStruct(a.shape, BF16) for a in to_bf16]],
        compiler_params=_params(("parallel",)),
        name="ada",
    )(c, w, b.reshape(1, n_out), *to_bf16)


def _inproj_kernel(x_ref, sh_ref, sc_ref, w_ref, o_ref):
    d = x_ref.shape[1]
    h = (_ln0(x_ref[...]) * (1.0 + sc_ref[0]) + sh_ref[0]).astype(BF16)
    for j in range(N_COLS):
        cols = slice(j * d, (j + 1) * d)
        acc = jnp.dot(h, w_ref[:, cols], preferred_element_type=F32)
        if j >= COL_GR:
            acc = jax.nn.sigmoid(acc)
        o_ref[:, cols] = acc.astype(BF16)


def _inproj(x2, sh, sc, w, seq, tm):
    n, d = x2.shape
    per_b = seq // tm
    return pl.pallas_call(
        _inproj_kernel,
        grid=(n // tm,),
        in_specs=[pl.BlockSpec((tm, d), lambda i: (i, 0)),
                  pl.BlockSpec((1, 1, d), lambda i: (i // per_b, 0, 0)),
                  pl.BlockSpec((1, 1, d), lambda i: (i // per_b, 0, 0)),
                  pl.BlockSpec(w.shape, lambda i: (0, 0))],
        out_specs=pl.BlockSpec((tm, N_COLS * d), lambda i: (i, 0)),
        out_shape=jax.ShapeDtypeStruct((n, N_COLS * d), BF16),
        compiler_params=_params(("parallel",)),
        name="inproj",
    )(x2, sh, sc, w)


def _stream_pitch(n_slab):
    assert n_slab % SUBLANES == 0
    return n_slab if (n_slab // SUBLANES) % 2 else n_slab + SUBLANES


def _rnn_kernel(xr_ref, yr_ref, g_ref, cw_ref, cb_ref, wa_ref, ba_ref, wi_ref, bi_ref,
                lam_ref, p_ref, o_ref, x_scr, y_scr, g_scr, o_scr, tail_scr, h_scr):
    t = pl.program_id(1)
    tt, d = xr_ref.shape
    n_slab = tt // SUBLANES
    n_hist = CONV_W - 1

    @pl.when(t == 0)
    def _():
        tail_scr[...] = jnp.zeros_like(tail_scr)
        h_scr[...] = jnp.zeros_like(h_scr)

    def slab(v, j):
        return v[j * SUBLANES:(j + 1) * SUBLANES, :]

    n_lane = d // LANES
    pitch = _stream_pitch(n_slab)

    def permuted(src_ref, scr):
        x = src_ref[...].astype(F32)
        for c in range(n_lane):
            for s in range(SUBLANES):
                scr[c, s * pitch:s * pitch + n_slab, :] = x[s * n_slab:(s + 1) * n_slab, c * LANES:(c + 1) * LANES]
        return [jnp.concatenate([scr[c, pl.ds(j, SUBLANES, stride=pitch), :] for c in range(n_lane)],
                                axis=-1) for j in range(n_slab)]

    sub = lax.broadcasted_iota(jnp.int32, (SUBLANES, d), 0)

    xs = permuted(xr_ref, x_scr)
    wrapped = []
    for i in range(n_hist):
        cur = xs[n_slab - n_hist + i]
        prev = slab(tail_scr, i)
        wrapped.append(pltpu.roll(jnp.where(sub == SUBLANES - 1, prev, cur), 1, 0))
        tail_scr[i * SUBLANES:(i + 1) * SUBLANES, :] = cur

    def x_at(j):
        return xs[j] if j >= 0 else wrapped[j + n_hist]

    cw = cw_ref[...]
    w = [cw[i:i + 1, :] for i in range(CONV_W)]
    cb = cb_ref[...]
    xc = jnp.concatenate(
        [cb + w[3] * xs[j] + w[2] * x_at(j - 1) + w[1] * x_at(j - 2) + w[0] * x_at(j - 3)
         for j in range(n_slab)], axis=0)

    xcb = xc.astype(BF16)
    n_g = d // MXU_TILE
    ra = jnp.concatenate(
        [jnp.dot(xcb[:, g * MXU_TILE:(g + 1) * MXU_TILE], wa_ref[g], preferred_element_type=F32)
         for g in range(n_g)], axis=-1)
    ri = jnp.concatenate(
        [jnp.dot(xcb[:, g * MXU_TILE:(g + 1) * MXU_TILE], wi_ref[g], preferred_element_type=F32)
         for g in range(n_g)], axis=-1)
    r = jax.nn.sigmoid(ra + ba_ref[...])
    gi = jax.nn.sigmoid(ri + bi_ref[...])
    z = -lam_ref[...]
    softplus = jnp.maximum(z, 0.0) + jnp.log1p(jnp.exp(-jnp.abs(z)))
    log_a = (-LRU_C) * r * softplus
    a = jnp.exp(log_a)
    mult = jnp.sqrt(jnp.maximum(1.0 - a * a, 0.0))
    u = xc * gi * mult

    h_loc, p_loc = [slab(u, 0)], [slab(a, 0)]
    for j in range(1, n_slab):
        aj = slab(a, j)
        h_loc.append(aj * h_loc[-1] + slab(u, j))
        p_loc.append(aj * p_loc[-1])
    pa, hb = p_loc[-1], h_loc[-1]
    step = 1
    while step < SUBLANES:
        valid = sub >= step
        hb = jnp.where(valid, pa * pltpu.roll(hb, step, 0) + hb, hb)
        pa = jnp.where(valid, pa * pltpu.roll(pa, step, 0), pa)
        step *= 2
    carry = h_scr[...]
    end = pa * carry + hb
    start = jnp.where(sub == 0, carry, pltpu.roll(end, 1, 0))
    h_scr[...] = end[SUBLANES - 1:SUBLANES, :]
    h = jnp.concatenate([h_loc[j] + p_loc[j] * start for j in range(n_slab)], axis=0)

    ys = jnp.concatenate(permuted(yr_ref, y_scr), axis=0)
    gs = jnp.concatenate(permuted(g_ref, g_scr), axis=0)
    ya = (h * jax.nn.gelu(ys)).astype(BF16)
    out = gs * jnp.dot(ya, p_ref[...], preferred_element_type=F32)
    for j in range(n_slab):
        for c in range(n_lane):
            o_scr[c, pl.ds(j, SUBLANES, stride=pitch), :] = slab(out, j)[:, c * LANES:(c + 1) * LANES]
    o_ref[...] = jnp.concatenate(
        [jnp.concatenate([o_scr[c, s * pitch:s * pitch + n_slab, :] for s in range(SUBLANES)], axis=0)
         for c in range(n_lane)], axis=-1).astype(BF16)


def _rnn(proj, cw, cb, wa, ba, wi, bi, lam, p_rnn, bsz, seq, tt):
    n = proj.shape[0]
    d = D_MODEL
    per_b = seq // tt
    row = lambda b, t: b * per_b + t
    vec = pl.BlockSpec((1, d), lambda b, t: (0, 0))
    gate_w = pl.BlockSpec(wa.shape, lambda b, t: (0, 0, 0))
    tile = pltpu.VMEM((d // LANES, SUBLANES * _stream_pitch(tt // SUBLANES), LANES), F32)
    return pl.pallas_call(
        _rnn_kernel,
        grid=(bsz, per_b),
        in_specs=[pl.BlockSpec((tt, d), lambda b, t: (row(b, t), COL_X)),
                  pl.BlockSpec((tt, d), lambda b, t: (row(b, t), COL_Y)),
                  pl.BlockSpec((tt, d), lambda b, t: (row(b, t), COL_GR)),
                  pl.BlockSpec((CONV_W, d), lambda b, t: (0, 0)),
                  vec, gate_w, vec, gate_w, vec, vec,
                  pl.BlockSpec((d, d), lambda b, t: (0, 0))],
        out_specs=pl.BlockSpec((tt, d), lambda b, t: (row(b, t), 0)),
        out_shape=jax.ShapeDtypeStruct((n, d), BF16),
        scratch_shapes=[tile, tile, tile, tile,
                        pltpu.VMEM(((CONV_W - 1) * SUBLANES, d), F32), pltpu.VMEM((1, d), F32)],
        compiler_params=_params(("parallel", "arbitrary")),
        name="rnn",
    )(proj, proj, proj, cw, cb, wa, ba, wi, bi, lam, p_rnn)


def _attn_kernel(q_ref, k_ref, v_ref, lq1_ref, lk1_ref, lq2_ref, lk2_ref, sg_ref, *rest, tq):
    n_cast = len(rest) // 2
    o_ref = rest[n_cast]
    for src, dst in zip(rest[:n_cast], rest[n_cast + 1:]):
        dst[...] = src[...].astype(BF16)
    seq = k_ref.shape[0]
    lam = (jnp.exp(jnp.sum(lq1_ref[...] * lk1_ref[...], axis=-1, keepdims=True))
           - jnp.exp(jnp.sum(lq2_ref[...] * lk2_ref[...], axis=-1, keepdims=True)) + LAM_INIT)
    nt = (((1,), (1,)), ((), ()))
    scale = DIFF_DH ** -0.5
    lane = lax.broadcasted_iota(jnp.int32, (tq, HEAD_W), 1)
    diag = (lax.broadcasted_iota(jnp.int32, (tq, tq), 1) // CHUNK
            <= lax.broadcasted_iota(jnp.int32, (tq, tq), 0) // CHUNK)

    diag2 = jnp.concatenate([diag, diag], axis=0)

    for qi in reversed(range(seq // tq)):
        kv = (qi + 1) * tq
        q = q_ref[qi * tq:(qi + 1) * tq, :] * scale
        zero = jnp.zeros_like(q)
        qq = jnp.concatenate([jnp.where(lane < DIFF_DH, q, zero), jnp.where(lane >= DIFF_DH, q, zero)], axis=0)
        s = lax.dot_general(qq, k_ref[0:kv, :], nt, preferred_element_type=F32)
        s_diag = jnp.where(diag2, s[:, kv - tq:], -jnp.inf)
        s = s_diag if kv == tq else jnp.concatenate([s[:, :kv - tq], s_diag], axis=1)
        e = jnp.exp(s - jnp.max(s, axis=-1, keepdims=True))
        pv = jnp.dot(e.astype(BF16), v_ref[0:kv, :], preferred_element_type=F32)
        pv = pv / jnp.sum(e, axis=-1, keepdims=True)
        o = pv[:tq] - lam * pv[tq:]
        o = o * lax.rsqrt(jnp.mean(o * o, axis=-1, keepdims=True) + RMS_EPS) * sg_ref[...]
        o_ref[qi * tq:(qi + 1) * tq, :] = (o * (1.0 - LAM_INIT)).astype(BF16)


def _attn(proj, lq1, lk1, lq2, lk2, sg, to_bf16, bsz, seq, tq):
    n = proj.shape[0]
    steps = bsz * N_HEADS

    def slab(a):
        assert a.shape[0] % steps == 0
        return pl.BlockSpec((a.shape[0] // steps,) + a.shape[1:], lambda b, h: (b * N_HEADS + h, 0, 0))

    hb = D_MODEL // HEAD_W
    lam_spec = pl.BlockSpec((1, DIFF_DH), lambda b, h: (0, 0))
    return pl.pallas_call(
        functools.partial(_attn_kernel, tq=tq),
        grid=(bsz, N_HEADS),
        in_specs=[pl.BlockSpec((seq, HEAD_W), lambda b, h: (b, COL_Q * hb + h)),
                  pl.BlockSpec((seq, HEAD_W), lambda b, h: (b, COL_K * hb + h)),
                  pl.BlockSpec((seq, HEAD_W), lambda b, h: (b, COL_V * hb + h)),
                  lam_spec, lam_spec, lam_spec, lam_spec,
                  pl.BlockSpec((1, HEAD_W), lambda b, h: (0, 0)),
                  *[slab(a) for a in to_bf16]],
        out_specs=[pl.BlockSpec((seq, HEAD_W), lambda b, h: (b, h)), *[slab(a) for a in to_bf16]],
        out_shape=[jax.ShapeDtypeStruct((n, D_MODEL), BF16),
                   *[jax.ShapeDtypeStruct(a.shape, BF16) for a in to_bf16]],
        compiler_params=_params(("parallel", "parallel")),
        name="attn",
    )(proj, proj, proj, lq1, lk1, lq2, lk2, sg, *to_bf16)


def _postmix_kernel(yb_ref, ma_ref, ga_ref, x_ref, g1_ref, sh_ref, sc_ref, lg_ref, lb_ref,
                    pa_ref, wo_ref, wr_ref, x1_ref, u_ref, lt_ref):
    tm = x_ref.shape[0]

    def project(rows):
        branch_b = jnp.dot(yb_ref[rows, :], pa_ref[...], preferred_element_type=F32)
        merged = ma_ref[rows, :].astype(F32) + ga_ref[rows, :].astype(F32) * branch_b
        return jnp.dot(merged.astype(BF16), wo_ref[...], preferred_element_type=F32)

    def normalize(rows, mix):
        x1 = _ln0(ALPHA * x_ref[rows, :] + g1_ref[0] * mix) * lg_ref[...] + lb_ref[...]
        x1_ref[rows, :] = x1
        u = _ln0(x1) * (1.0 + sc_ref[0]) + sh_ref[0]
        u_ref[rows, :] = u.astype(BF16)
        u_hi = u.astype(BF16)
        u_lo = (u - u_hi.astype(F32)).astype(BF16)
        a = jnp.dot(u_hi, wr_ref[...], preferred_element_type=F32)
        b = jnp.dot(u_lo, wr_ref[...], preferred_element_type=F32)
        lt_ref[rows, :] = a + pltpu.roll(a, N_EXPERTS, 1) + b

    chunks = [slice(r0, r0 + POSTMIX_CHUNK) for r0 in range(0, tm, POSTMIX_CHUNK)]
    mix = project(chunks[0])
    for prev, rows in zip(chunks[:-1], chunks[1:]):
        nxt = project(rows)
        normalize(prev, mix)
        mix = nxt
    normalize(chunks[-1], mix)


def _postmix(yb, ma, proj, x2, g1, sh2, sc2, ln_g, ln_b, p_attn, w_out, wr_hilo, seq, tm):
    n, d = x2.shape
    per_b = seq // tm
    row = pl.BlockSpec((tm, d), lambda i: (i, 0))
    mod = pl.BlockSpec((1, 1, d), lambda i: (i // per_b, 0, 0))
    vec = pl.BlockSpec((1, d), lambda i: (0, 0))
    mat = pl.BlockSpec((d, d), lambda i: (0, 0))
    return pl.pallas_call(
        _postmix_kernel,
        grid=(n // tm,),
        in_specs=[row, row, pl.BlockSpec((tm, d), lambda i: (i, COL_GA)), row,
                  mod, mod, mod, vec, vec, mat, mat,
                  pl.BlockSpec((d, 2 * N_EXPERTS), lambda i: (0, 0))],
        out_specs=[row, row, pl.BlockSpec((tm, 2 * N_EXPERTS), lambda i: (i, 0))],
        out_shape=[jax.ShapeDtypeStruct((n, d), F32),
                   jax.ShapeDtypeStruct((n, d), BF16),
                   jax.ShapeDtypeStruct((n, 2 * N_EXPERTS), F32)],
        compiler_params=_params(("parallel",)),
        name="postmix",
    )(yb, ma, proj, x2, g1, sh2, sc2, ln_g, ln_b, p_attn, w_out, wr_hilo)


def _route_kernel(lt_ref, bias_ref, gate_ref):
    tn = lt_ref.shape[0]
    shape3 = (N_GROUPS, GROUP_SIZE, tn)
    neg = -jnp.inf
    s = jax.nn.sigmoid(lt_ref[...].T[:N_EXPERTS, :])
    sel = (s + bias_ref[...]).reshape(shape3)
    s = s.reshape(shape3)
    j_idx = lax.broadcasted_iota(jnp.int32, shape3, 1).astype(F32)
    g_idx = lax.broadcasted_iota(jnp.int32, shape3, 0).astype(F32)
    e_idx = g_idx * GROUP_SIZE + j_idx

    m1 = jnp.max(sel, axis=1, keepdims=True)
    first = jnp.min(jnp.where(sel == m1, j_idx, float(GROUP_SIZE)), axis=1, keepdims=True)
    m2 = jnp.max(jnp.where(j_idx == first, neg, sel), axis=1, keepdims=True)
    gscore = m1 + m2

    gi = lax.broadcasted_iota(jnp.int32, gscore.shape, 0).astype(F32)
    gkeep = jnp.zeros(gscore.shape, jnp.bool_)
    for _ in range(TOPK_GROUPS):
        m = jnp.max(gscore, axis=0, keepdims=True)
        pick = jnp.min(jnp.where(gscore == m, gi, float(N_GROUPS)), axis=0, keepdims=True)
        hit = gi == pick
        gkeep = jnp.logical_or(gkeep, hit)
        gscore = jnp.where(hit, neg, gscore)
    sel = jnp.where(gkeep, sel, neg)

    picked = jnp.zeros(shape3, F32)
    for _ in range(TOP_K):
        m = jnp.max(jnp.max(sel, axis=1, keepdims=True), axis=0, keepdims=True)
        cand = jnp.where(sel == m, e_idx, float(N_EXPERTS))
        pick = jnp.min(jnp.min(cand, axis=1, keepdims=True), axis=0, keepdims=True)
        hit = e_idx == pick
        picked = jnp.where(hit, s, picked)
        sel = jnp.where(hit, neg, sel)
    total = jnp.sum(jnp.sum(picked, axis=1, keepdims=True), axis=0, keepdims=True)
    gate = picked / total * ROUTED_SCALE
    gate = gate.reshape(N_EXPERTS, tn)
    shared = (lax.broadcasted_iota(jnp.int32, gate.shape, 0) == 0).astype(F32)
    gate_ref[...] = jnp.concatenate([gate, shared], axis=0).T


def _route(logits, bias, tn):
    n, w = logits.shape
    e = bias.shape[0]
    return pl.pallas_call(
        _route_kernel,
        grid=(n // tn,),
        in_specs=[pl.BlockSpec((tn, w), lambda i: (i, 0)),
                  pl.BlockSpec((e, 1), lambda i: (0, 0))],
        out_specs=pl.BlockSpec((tn, w), lambda i: (i, 0)),
        out_shape=jax.ShapeDtypeStruct((n, w), F32),
        compiler_params=_params(("parallel",)),
        name="route",
    )(logits, bias.reshape(e, 1))


def _moe_kernel(*refs, eg):
    u_ref, gate_ref, x1_ref, g2_ref = refs[:4]
    wg_refs, wu_refs, wd_refs = refs[4:4 + eg], refs[4 + eg:4 + 2 * eg], refs[4 + 2 * eg:4 + 3 * eg]
    sg_ref, su_ref, sd_ref, lg_ref, lb_ref, o_ref, acc_scr = refs[4 + 3 * eg:]
    j = pl.program_id(1)
    last = j == pl.num_programs(1) - 1

    @pl.when(j == 0)
    def _():
        acc_scr[...] = jnp.zeros_like(acc_scr)

    u = u_ref[...]
    gate = gate_ref[...]
    lane = lax.broadcasted_iota(jnp.int32, gate.shape, 1)
    acts, downs = [], []
    for el in range(eg):
        wg, wu, wd = wg_refs[el][0], wu_refs[el][0], wd_refs[el][0]
        if el == eg - 1:
            wg = jnp.where(last, sg_ref[...], wg)
            wu = jnp.where(last, su_ref[...], wu)
            wd = jnp.where(last, sd_ref[...], wd)
        gk = jnp.sum(jnp.where(lane == j * eg + el, gate, 0.0), axis=-1, keepdims=True)
        hg = jnp.dot(u, wg, preferred_element_type=F32)
        hu = jnp.dot(u, wu, preferred_element_type=F32)
        acts.append((_silu(hg) * hu * gk).astype(BF16))
        downs.append(wd)
    acc_scr[...] += jnp.dot(jnp.concatenate(acts, axis=-1), jnp.concatenate(downs, axis=0),
                            preferred_element_type=F32)

    @pl.when(last)
    def _():
        z = ALPHA * x1_ref[...] + g2_ref[0] * acc_scr[...]
        o_ref[...] = _ln0(z) * lg_ref[...] + lb_ref[...]


def _moe(u, gate, x1, g2, wg, wu, wd, sg, su, sd, ln_g, ln_b, seq, tm, eg):
    n, d = x1.shape
    n_e, _, f = wg.shape
    assert (n_e + 1) % eg == 0 and n_e < gate.shape[1] and sg.shape == (d, f)
    per_b = seq // tm
    row = pl.BlockSpec((tm, d), lambda i, j: (i, 0))
    vec = pl.BlockSpec((1, d), lambda i, j: (0, 0))
    const = lambda a: pl.BlockSpec(a.shape, lambda i, j: (0, 0))

    def expert(shape, el):
        return pl.BlockSpec((1,) + shape, lambda i, j: (jnp.minimum(j * eg + el, n_e - 1), 0, 0))

    slots = range(eg)
    return pl.pallas_call(
        functools.partial(_moe_kernel, eg=eg),
        grid=(n // tm, (n_e + 1) // eg),
        in_specs=[row, pl.BlockSpec((tm, gate.shape[1]), lambda i, j: (i, 0)), row,
                  pl.BlockSpec((1, 1, d), lambda i, j: (i // per_b, 0, 0)),
                  *[expert((d, f), el) for el in slots], *[expert((d, f), el) for el in slots],
                  *[expert((f, d), el) for el in slots],
                  const(sg), const(su), const(sd), vec, vec],
        out_specs=row,
        out_shape=jax.ShapeDtypeStruct((n, d), F32),
        scratch_shapes=[pltpu.VMEM((tm, d), F32)],
        compiler_params=_params(("parallel", "arbitrary")),
        name="moe",
    )(u, gate, x1, g2, *[wg] * eg, *[wu] * eg, *[wd] * eg, sg, su, sd, ln_g, ln_b)


def _block_diag_tiles(w):
    nb, bw, _ = w.shape
    per = MXU_TILE // bw
    w = w.reshape(nb // per, per, bw, bw)
    eye = jnp.eye(per, dtype=w.dtype)
    tiles = jnp.einsum("gpij,pq->gpiqj", w, eye)
    return tiles.reshape(nb // per, MXU_TILE, MXU_TILE)


def kernel(x, c, w_ada, b_ada, w_in, conv_w, conv_b, lru_w_a, lru_b_a, lru_w_i, lru_b_i, lru_lambda,
           lam_q1, lam_k1, lam_q2, lam_k2, subln_g, p_rnn, p_attn, w_out, ln1_g, ln1_b,
           w_router, router_bias, we_gate, we_up, we_down, ws_gate, ws_up, ws_down, ln2_g, ln2_b):
    bsz, seq, d = x.shape
    n = bsz * seq
    x2 = x.reshape(n, d)
    l = 0

    ada, w_in_bf, p_rnn_bf, p_attn_bf, w_out_bf, sg_bf, su_bf, sd_bf = _ada(
        c, w_ada[l], b_ada[l], [w_in[l], p_rnn[l], p_attn[l], w_out[l], ws_gate[l], ws_up[l], ws_down[l]])
    sh1, sc1, g1, sh2, sc2, g2 = [ada[:, i * d:(i + 1) * d].reshape(bsz, 1, d) for i in range(6)]

    proj = _inproj(x2, sh1, sc1, w_in_bf, seq, tm=TILES["inproj_rows"])

    ma = _rnn(proj, conv_w[l], conv_b[l].reshape(1, d),
              _block_diag_tiles(lru_w_a[l]).astype(BF16), lru_b_a[l].reshape(1, d),
              _block_diag_tiles(lru_w_i[l]).astype(BF16), lru_b_i[l].reshape(1, d),
              lru_lambda[l].reshape(1, d), p_rnn_bf, bsz, seq, tt=TILES["rnn_rows"])

    yb, wg_bf, wu_bf, wd_bf = _attn(proj, lam_q1[l].reshape(1, -1), lam_k1[l].reshape(1, -1),
                                    lam_q2[l].reshape(1, -1), lam_k2[l].reshape(1, -1),
                                    subln_g[l].reshape(1, -1), [we_gate[l], we_up[l], we_down[l]],
                                    bsz, seq, tq=TILES["attn_q_rows"])

    wr_hi = w_router[l].astype(BF16)
    wr_lo = (w_router[l] - wr_hi.astype(F32)).astype(BF16)
    x1, u, logits = _postmix(yb, ma, proj, x2, g1, sh2, sc2, ln1_g[l].reshape(1, d),
                             ln1_b[l].reshape(1, d), p_attn_bf,
                             w_out_bf, jnp.concatenate([wr_hi, wr_lo], axis=1), seq,
                             tm=TILES["postmix_rows"])

    gate = _route(logits, router_bias[l], tn=TILES["route_tokens"])

    out = _moe(u, gate, x1, g2, wg_bf, wu_bf, wd_bf,
               sg_bf, su_bf, sd_bf,
               ln2_g[l].reshape(1, d), ln2_b[l].reshape(1, d),
               seq, tm=TILES["moe_rows"], eg=TILES["moe_experts"])
    return out.reshape(bsz, seq, d)
```

```python
import functools
import math

import jax
import jax.numpy as jnp
from jax import lax
from jax.experimental import pallas as pl
from jax.experimental.pallas import tpu as pltpu

F32 = jnp.float32
BF16 = jnp.bfloat16

D_MODEL = 1024
CHUNK = 64
CONV_W = 4
LRU_C = 8.0
N_HEADS = 8
DIFF_DH = 64
HEAD_W = 2 * DIFF_DH
N_EXPERTS = 64
TOP_K = 8
N_GROUPS = 8
GROUP_SIZE = N_EXPERTS // N_GROUPS
TOPK_GROUPS = 4
ROUTED_SCALE = 2.5
DEPTH = 1
ALPHA = (2.0 * DEPTH) ** 0.25
LN_EPS = 1e-5
RMS_EPS = 1e-5
LAM_INIT = 0.8 - 0.6 * math.exp(-0.3 * 0)

COL_X, COL_Y, COL_Q, COL_K, COL_V, COL_GR, COL_GA = range(7)
N_COLS = 7

V7X_VMEM_LIMIT = 56 * 1024 * 1024
MXU_TILE = 256
SUBLANES = 8
LANES = 128
ADA_STEPS = 8
POSTMIX_CHUNK = 512

TILES = dict(
    inproj_rows=512,
    rnn_rows=512,
    attn_q_rows=256,
    postmix_rows=1024,
    route_tokens=1024,
    moe_rows=1024,
    moe_experts=5,
)


def _ln0(x):
    mu = jnp.mean(x, axis=-1, keepdims=True)
    xc = x - mu
    var = jnp.mean(xc * xc, axis=-1, keepdims=True)
    return xc * lax.rsqrt(var + LN_EPS)


def _silu(x):
    return x * jax.nn.sigmoid(x)


def _params(sem):
    return pltpu.CompilerParams(dimension_semantics=sem, vmem_limit_bytes=V7X_VMEM_LIMIT)


def _ada_kernel(c_ref, w_ref, b_ref, *rest):
    n_cast = len(rest) // 2
    o_ref = rest[n_cast]
    for src, dst in zip(rest[:n_cast], rest[n_cast + 1:]):
        dst[...] = src[...].astype(BF16)
    cond = _silu(c_ref[...])
    o_ref[...] = jnp.dot(cond, w_ref[...], preferred_element_type=F32,
                         precision=lax.Precision.HIGHEST) + b_ref[...]


def _ada(c, w, b, to_bf16):
    bsz, d = c.shape
    n_out = w.shape[1]
    steps = ADA_STEPS
    tn = n_out // steps

    def slab(a):
        assert a.shape[0] % (steps * 2 * SUBLANES) == 0
        return pl.BlockSpec((a.shape[0] // steps, a.shape[1]), lambda j: (j, 0))

    return pl.pallas_call(
        _ada_kernel,
        grid=(steps,),
        in_specs=[pl.BlockSpec((bsz, d), lambda j: (0, 0)),
                  pl.BlockSpec((d, tn), lambda j: (0, j)),
                  pl.BlockSpec((1, tn), lambda j: (0, j)),
                  *[slab(a) for a in to_bf16]],
        out_specs=[pl.BlockSpec((bsz, tn), lambda j: (0, j)), *[slab(a) for a in to_bf16]],
        out_shape=[jax.ShapeDtypeStruct((bsz, n_out), F32),
                   *[jax.ShapeDtypeStruct(a.shape, BF16) for a in to_bf16]],
        compiler_params=_params(("parallel",)),
        name="ada",
    )(c, w, b.reshape(1, n_out), *to_bf16)


def _inproj_kernel(x_ref, sh_ref, sc_ref, w_ref, o_ref):
    d = x_ref.shape[1]
    h = (_ln0(x_ref[...]) * (1.0 + sc_ref[0]) + sh_ref[0]).astype(BF16)
    for j in range(N_COLS):
        cols = slice(j * d, (j + 1) * d)
        acc = jnp.dot(h, w_ref[:, cols], preferred_element_type=F32)
        if j >= COL_GR:
            acc = jax.nn.sigmoid(acc)
        o_ref[:, cols] = acc.astype(BF16)


def _inproj(x2, sh, sc, w, seq, tm):
    n, d = x2.shape
    per_b = seq // tm
    return pl.pallas_call(
        _inproj_kernel,
        grid=(n // tm,),
        in_specs=[pl.BlockSpec((tm, d), lambda i: (i, 0)),
                  pl.BlockSpec((1, 1, d), lambda i: (i // per_b, 0, 0)),
                  pl.BlockSpec((1, 1, d), lambda i: (i // per_b, 0, 0)),
                  pl.BlockSpec(w.shape, lambda i: (0, 0))],
        out_specs=pl.BlockSpec((tm, N_COLS * d), lambda i: (i, 0)),
        out_shape=jax.ShapeDtypeStruct((n, N_COLS * d), BF16),
        compiler_params=_params(("parallel",)),
        name="inproj",
    )(x2, sh, sc, w)


def _stream_pitch(n_slab):
    assert n_slab % SUBLANES == 0
    return n_slab if (n_slab // SUBLANES) % 2 else n_slab + SUBLANES


def _rnn_kernel(xr_ref, yr_ref, g_ref, cw_ref, cb_ref, wa_ref, ba_ref, wi_ref, bi_ref,
                lam_ref, p_ref, o_ref, x_scr, y_scr, g_scr, o_scr, tail_scr, h_scr):
    t = pl.program_id(1)
    tt, d = xr_ref.shape
    n_slab = tt // SUBLANES
    n_hist = CONV_W - 1

    @pl.when(t == 0)
    def _():
        tail_scr[...] = jnp.zeros_like(tail_scr)
        h_scr[...] = jnp.zeros_like(h_scr)

    def slab(v, j):
        return v[j * SUBLANES:(j + 1) * SUBLANES, :]

    n_lane = d // LANES
    pitch = _stream_pitch(n_slab)

    def permuted(src_ref, scr):
        x = src_ref[...].astype(F32)
        for c in range(n_lane):
            for s in range(SUBLANES):
                scr[c, s * pitch:s * pitch + n_slab, :] = x[s * n_slab:(s + 1) * n_slab, c * LANES:(c + 1) * LANES]
        return [jnp.concatenate([scr[c, pl.ds(j, SUBLANES, stride=pitch), :] for c in range(n_lane)],
                                axis=-1) for j in range(n_slab)]

    sub = lax.broadcasted_iota(jnp.int32, (SUBLANES, d), 0)

    xs = permuted(xr_ref, x_scr)
    wrapped = []
    for i in range(n_hist):
        cur = xs[n_slab - n_hist + i]
        prev = slab(tail_scr, i)
        wrapped.append(pltpu.roll(jnp.where(sub == SUBLANES - 1, prev, cur), 1, 0))
        tail_scr[i * SUBLANES:(i + 1) * SUBLANES, :] = cur

    def x_at(j):
        return xs[j] if j >= 0 else wrapped[j + n_hist]

    cw = cw_ref[...]
    w = [cw[i:i + 1, :] for i in range(CONV_W)]
    cb = cb_ref[...]
    xc = jnp.concatenate(
        [sum((w[CONV_W - 1 - k] * x_at(j - k) for k in range(CONV_W)), start=cb) for j in range(n_slab)],
        axis=0)

    xcb = xc.astype(BF16)
    n_g = d // MXU_TILE
    ra = jnp.concatenate(
        [jnp.dot(xcb[:, g * MXU_TILE:(g + 1) * MXU_TILE], wa_ref[g], preferred_element_type=F32)
         for g in range(n_g)], axis=-1)
    ri = jnp.concatenate(
        [jnp.dot(xcb[:, g * MXU_TILE:(g + 1) * MXU_TILE], wi_ref[g], preferred_element_type=F32)
         for g in range(n_g)], axis=-1)
    r = jax.nn.sigmoid(ra + ba_ref[...])
    gi = jax.nn.sigmoid(ri + bi_ref[...])
    z = -lam_ref[...]
    softplus = jnp.maximum(z, 0.0) + jnp.log1p(jnp.exp(-jnp.abs(z)))
    log_a = (-LRU_C) * r * softplus
    a = jnp.exp(log_a)
    mult = jnp.sqrt(jnp.maximum(1.0 - a * a, 0.0))
    u = xc * gi * mult

    h_loc, p_loc = [slab(u, 0)], [slab(a, 0)]
    for j in range(1, n_slab):
        aj = slab(a, j)
        h_loc.append(aj * h_loc[-1] + slab(u, j))
        p_loc.append(aj * p_loc[-1])
    pa, hb = p_loc[-1], h_loc[-1]
    step = 1
    while step < SUBLANES:
        valid = sub >= step
        hb = jnp.where(valid, pa * pltpu.roll(hb, step, 0) + hb, hb)
        pa = jnp.where(valid, pa * pltpu.roll(pa, step, 0), pa)
        step *= 2
    carry = h_scr[...]
    end = pa * carry + hb
    start = jnp.where(sub == 0, carry, pltpu.roll(end, 1, 0))
    h_scr[...] = end[SUBLANES - 1:SUBLANES, :]
    h = jnp.concatenate([h_loc[j] + p_loc[j] * start for j in range(n_slab)], axis=0)

    ys = jnp.concatenate(permuted(yr_ref, y_scr), axis=0)
    gs = jnp.concatenate(permuted(g_ref, g_scr), axis=0)
    ya = (h * jax.nn.gelu(ys)).astype(BF16)
    out = gs * jnp.dot(ya, p_ref[...], preferred_element_type=F32)
    for j in range(n_slab):
        for c in range(n_lane):
            o_scr[c, pl.ds(j, SUBLANES, stride=pitch), :] = slab(out, j)[:, c * LANES:(c + 1) * LANES]
    o_ref[...] = jnp.concatenate(
        [jnp.concatenate([o_scr[c, s * pitch:s * pitch + n_slab, :] for s in range(SUBLANES)], axis=0)
         for c in range(n_lane)], axis=-1).astype(BF16)


def _rnn(proj, cw, cb, wa, ba, wi, bi, lam, p_rnn, bsz, seq, tt):
    n = proj.shape[0]
    d = D_MODEL
    per_b = seq // tt
    row = lambda b, t: b * per_b + t
    vec = pl.BlockSpec((1, d), lambda b, t: (0, 0))
    gate_w = pl.BlockSpec(wa.shape, lambda b, t: (0, 0, 0))
    tile = pltpu.VMEM((d // LANES, SUBLANES * _stream_pitch(tt // SUBLANES), LANES), F32)
    return pl.pallas_call(
        _rnn_kernel,
        grid=(bsz, per_b),
        in_specs=[pl.BlockSpec((tt, d), lambda b, t: (row(b, t), COL_X)),
                  pl.BlockSpec((tt, d), lambda b, t: (row(b, t), COL_Y)),
                  pl.BlockSpec((tt, d), lambda b, t: (row(b, t), COL_GR)),
                  pl.BlockSpec((CONV_W, d), lambda b, t: (0, 0)),
                  vec, gate_w, vec, gate_w, vec, vec,
                  pl.BlockSpec((d, d), lambda b, t: (0, 0))],
        out_specs=pl.BlockSpec((tt, d), lambda b, t: (row(b, t), 0)),
        out_shape=jax.ShapeDtypeStruct((n, d), BF16),
        scratch_shapes=[tile, tile, tile, tile,
                        pltpu.VMEM(((CONV_W - 1) * SUBLANES, d), F32), pltpu.VMEM((1, d), F32)],
        compiler_params=_params(("parallel", "arbitrary")),
        name="rnn",
    )(proj, proj, proj, cw, cb, wa, ba, wi, bi, lam, p_rnn)


def _attn_kernel(q_ref, k_ref, v_ref, lq1_ref, lk1_ref, lq2_ref, lk2_ref, sg_ref, *rest, tq):
    n_cast = len(rest) // 2
    o_ref = rest[n_cast]
    for src, dst in zip(rest[:n_cast], rest[n_cast + 1:]):
        dst[...] = src[...].astype(BF16)
    seq = k_ref.shape[0]
    lam = (jnp.exp(jnp.sum(lq1_ref[...] * lk1_ref[...], axis=-1, keepdims=True))
           - jnp.exp(jnp.sum(lq2_ref[...] * lk2_ref[...], axis=-1, keepdims=True)) + LAM_INIT)
    nt = (((1,), (1,)), ((), ()))
    scale = DIFF_DH ** -0.5
    lane = lax.broadcasted_iota(jnp.int32, (tq, HEAD_W), 1)
    diag = (lax.broadcasted_iota(jnp.int32, (tq, tq), 1) // CHUNK
            <= lax.broadcasted_iota(jnp.int32, (tq, tq), 0) // CHUNK)

    diag2 = jnp.concatenate([diag, diag], axis=0)

    for qi in reversed(range(seq // tq)):
        kv = (qi + 1) * tq
        q = q_ref[qi * tq:(qi + 1) * tq, :] * scale
        zero = jnp.zeros_like(q)
        qq = jnp.concatenate([jnp.where(lane < DIFF_DH, q, zero), jnp.where(lane >= DIFF_DH, q, zero)], axis=0)
        s = lax.dot_general(qq, k_ref[0:kv, :], nt, preferred_element_type=F32)
        s_diag = jnp.where(diag2, s[:, kv - tq:], -jnp.inf)
        s = s_diag if kv == tq else jnp.concatenate([s[:, :kv - tq], s_diag], axis=1)
        e = jnp.exp(s - jnp.max(s, axis=-1, keepdims=True))
        pv = jnp.dot(e.astype(BF16), v_ref[0:kv, :], preferred_element_type=F32)
        pv = pv / jnp.sum(e, axis=-1, keepdims=True)
        o = pv[:tq] - lam * pv[tq:]
        o = o * lax.rsqrt(jnp.mean(o * o, axis=-1, keepdims=True) + RMS_EPS) * sg_ref[...]
        o_ref[qi * tq:(qi + 1) * tq, :] = (o * (1.0 - LAM_INIT)).astype(BF16)


def _attn(proj, lq1, lk1, lq2, lk2, sg, to_bf16, bsz, seq, tq):
    n = proj.shape[0]
    steps = bsz * N_HEADS

    def slab(a):
        assert a.shape[0] % steps == 0
        return pl.BlockSpec((a.shape[0] // steps,) + a.shape[1:], lambda b, h: (b * N_HEADS + h, 0, 0))

    hb = D_MODEL // HEAD_W
    lam_spec = pl.BlockSpec((1, DIFF_DH), lambda b, h: (0, 0))
    return pl.pallas_call(
        functools.partial(_attn_kernel, tq=tq),
        grid=(bsz, N_HEADS),
        in_specs=[pl.BlockSpec((seq, HEAD_W), lambda b, h: (b, COL_Q * hb + h)),
                  pl.BlockSpec((seq, HEAD_W), lambda b, h: (b, COL_K * hb + h)),
                  pl.BlockSpec((seq, HEAD_W), lambda b, h: (b, COL_V * hb + h)),
                  lam_spec, lam_spec, lam_spec, lam_spec,
                  pl.BlockSpec((1, HEAD_W), lambda b, h: (0, 0)),
                  *[slab(a) for a in to_bf16]],
        out_specs=[pl.BlockSpec((seq, HEAD_W), lambda b, h: (b, h)), *[slab(a) for a in to_bf16]],
        out_shape=[jax.ShapeDtypeStruct((n, D_MODEL), BF16),
                   *[jax.ShapeDtypeStruct(a.shape, BF16) for a in to_bf16]],
        compiler_params=_params(("parallel", "parallel")),
        name="attn",
    )(proj, proj, proj, lq1, lk1, lq2, lk2, sg, *to_bf16)


def _postmix_kernel(yb_ref, ma_ref, ga_ref, x_ref, g1_ref, sh_ref, sc_ref, lg_ref, lb_ref,
                    pa_ref, wo_ref, wr_ref, x1_ref, u_ref, lt_ref):
    tm = x_ref.shape[0]

    def project(rows):
        branch_b = jnp.dot(yb_ref[rows, :], pa_ref[...], preferred_element_type=F32)
        merged = ma_ref[rows, :].astype(F32) + ga_ref[rows, :].astype(F32) * branch_b
        return jnp.dot(merged.astype(BF16), wo_ref[...], preferred_element_type=F32)

    def normalize(rows, mix):
        x1 = _ln0(ALPHA * x_ref[rows, :] + g1_ref[0] * mix) * lg_ref[...] + lb_ref[...]
        x1_ref[rows, :] = x1
        u = _ln0(x1) * (1.0 + sc_ref[0]) + sh_ref[0]
        u_ref[rows, :] = u.astype(BF16)
        u_hi = u.astype(BF16)
        u_lo = (u - u_hi.astype(F32)).astype(BF16)
        a = jnp.dot(u_hi, wr_ref[...], preferred_element_type=F32)
        b = jnp.dot(u_lo, wr_ref[...], preferred_element_type=F32)
        lt_ref[rows, :] = a + pltpu.roll(a, N_EXPERTS, 1) + b

    chunks = [slice(r0, r0 + POSTMIX_CHUNK) for r0 in range(0, tm, POSTMIX_CHUNK)]
    mix = project(chunks[0])
    for prev, rows in zip(chunks[:-1], chunks[1:]):
        nxt = project(rows)
        normalize(prev, mix)
        mix = nxt
    normalize(chunks[-1], mix)


def _postmix(yb, ma, proj, x2, g1, sh2, sc2, ln_g, ln_b, p_attn, w_out, wr_hilo, seq, tm):
    n, d = x2.shape
    per_b = seq // tm
    row = pl.BlockSpec((tm, d), lambda i: (i, 0))
    mod = pl.BlockSpec((1, 1, d), lambda i: (i // per_b, 0, 0))
    vec = pl.BlockSpec((1, d), lambda i: (0, 0))
    mat = pl.BlockSpec((d, d), lambda i: (0, 0))
    return pl.pallas_call(
        _postmix_kernel,
        grid=(n // tm,),
        in_specs=[row, row, pl.BlockSpec((tm, d), lambda i: (i, COL_GA)), row,
                  mod, mod, mod, vec, vec, mat, mat,
                  pl.BlockSpec((d, 2 * N_EXPERTS), lambda i: (0, 0))],
        out_specs=[row, row, pl.BlockSpec((tm, 2 * N_EXPERTS), lambda i: (i, 0))],
        out_shape=[jax.ShapeDtypeStruct((n, d), F32),
                   jax.ShapeDtypeStruct((n, d), BF16),
                   jax.ShapeDtypeStruct((n, 2 * N_EXPERTS), F32)],
        compiler_params=_params(("parallel",)),
        name="postmix",
    )(yb, ma, proj, x2, g1, sh2, sc2, ln_g, ln_b, p_attn, w_out, wr_hilo)


def _route_kernel(lt_ref, bias_ref, gate_ref):
    tn = lt_ref.shape[0]
    shape3 = (N_GROUPS, GROUP_SIZE, tn)
    neg = -jnp.inf
    s = jax.nn.sigmoid(lt_ref[...].T[:N_EXPERTS, :])
    sel = (s + bias_ref[...]).reshape(shape3)
    s = s.reshape(shape3)
    j_idx = lax.broadcasted_iota(jnp.int32, shape3, 1).astype(F32)
    g_idx = lax.broadcasted_iota(jnp.int32, shape3, 0).astype(F32)
    e_idx = g_idx * GROUP_SIZE + j_idx

    m1 = jnp.max(sel, axis=1, keepdims=True)
    first = jnp.min(jnp.where(sel == m1, j_idx, float(GROUP_SIZE)), axis=1, keepdims=True)
    m2 = jnp.max(jnp.where(j_idx == first, neg, sel), axis=1, keepdims=True)
    gscore = m1 + m2

    gi = lax.broadcasted_iota(jnp.int32, gscore.shape, 0).astype(F32)
    gkeep = jnp.zeros(gscore.shape, jnp.bool_)
    for _ in range(TOPK_GROUPS):
        m = jnp.max(gscore, axis=0, keepdims=True)
        pick = jnp.min(jnp.where(gscore == m, gi, float(N_GROUPS)), axis=0, keepdims=True)
        hit = gi == pick
        gkeep = jnp.logical_or(gkeep, hit)
        gscore = jnp.where(hit, neg, gscore)
    sel = jnp.where(gkeep, sel, neg)

    picked = jnp.zeros(shape3, F32)
    for _ in range(TOP_K):
        m = jnp.max(jnp.max(sel, axis=1, keepdims=True), axis=0, keepdims=True)
        cand = jnp.where(sel == m, e_idx, float(N_EXPERTS))
        pick = jnp.min(jnp.min(cand, axis=1, keepdims=True), axis=0, keepdims=True)
        hit = e_idx == pick
        picked = jnp.where(hit, s, picked)
        sel = jnp.where(hit, neg, sel)
    total = jnp.sum(jnp.sum(picked, axis=1, keepdims=True), axis=0, keepdims=True)
    gate = picked / total * ROUTED_SCALE
    gate = gate.reshape(N_EXPERTS, tn)
    shared = (lax.broadcasted_iota(jnp.int32, gate.shape, 0) == 0).astype(F32)
    gate_ref[...] = jnp.concatenate([gate, shared], axis=0).T


def _route(logits, bias, tn):
    n, w = logits.shape
    e = bias.shape[0]
    return pl.pallas_call(
        _route_kernel,
        grid=(n // tn,),
        in_specs=[pl.BlockSpec((tn, w), lambda i: (i, 0)),
                  pl.BlockSpec((e, 1), lambda i: (0, 0))],
        out_specs=pl.BlockSpec((tn, w), lambda i: (i, 0)),
        out_shape=jax.ShapeDtypeStruct((n, w), F32),
        compiler_params=_params(("parallel",)),
        name="route",
    )(logits, bias.reshape(e, 1))


def _moe_kernel(*refs, eg):
    u_ref, gate_ref, x1_ref, g2_ref = refs[:4]
    wg_refs, wu_refs, wd_refs = refs[4:4 + eg], refs[4 + eg:4 + 2 * eg], refs[4 + 2 * eg:4 + 3 * eg]
    sg_ref, su_ref, sd_ref, lg_ref, lb_ref, o_ref, acc_scr = refs[4 + 3 * eg:]
    j = pl.program_id(1)
    last = j == pl.num_programs(1) - 1

    @pl.when(j == 0)
    def _():
        acc_scr[...] = jnp.zeros_like(acc_scr)

    u = u_ref[...]
    gate = gate_ref[...]
    lane = lax.broadcasted_iota(jnp.int32, gate.shape, 1)
    acts, downs = [], []
    for el in range(eg):
        wg, wu, wd = wg_refs[el][0], wu_refs[el][0], wd_refs[el][0]
        if el == eg - 1:
            wg = jnp.where(last, sg_ref[...], wg)
            wu = jnp.where(last, su_ref[...], wu)
            wd = jnp.where(last, sd_ref[...], wd)
        gk = jnp.sum(jnp.where(lane == j * eg + el, gate, 0.0), axis=-1, keepdims=True)
        hg = jnp.dot(u, wg, preferred_element_type=F32)
        hu = jnp.dot(u, wu, preferred_element_type=F32)
        acts.append((_silu(hg) * hu * gk).astype(BF16))
        downs.append(wd)
    acc_scr[...] += jnp.dot(jnp.concatenate(acts, axis=-1), jnp.concatenate(downs, axis=0),
                            preferred_element_type=F32)

    @pl.when(last)
    def _():
        z = ALPHA * x1_ref[...] + g2_ref[0] * acc_scr[...]
        o_ref[...] = _ln0(z) * lg_ref[...] + lb_ref[...]


def _moe(u, gate, x1, g2, wg, wu, wd, sg, su, sd, ln_g, ln_b, seq, tm, eg):
    n, d = x1.shape
    n_e, _, f = wg.shape
    assert (n_e + 1) % eg == 0 and n_e < gate.shape[1] and sg.shape == (d, f)
    per_b = seq // tm
    row = pl.BlockSpec((tm, d), lambda i, j: (i, 0))
    vec = pl.BlockSpec((1, d), lambda i, j: (0, 0))
    const = lambda a: pl.BlockSpec(a.shape, lambda i, j: (0, 0))

    def expert(shape, el):
        return pl.BlockSpec((1,) + shape, lambda i, j: (jnp.minimum(j * eg + el, n_e - 1), 0, 0))

    slots = range(eg)
    return pl.pallas_call(
        functools.partial(_moe_kernel, eg=eg),
        grid=(n // tm, (n_e + 1) // eg),
        in_specs=[row, pl.BlockSpec((tm, gate.shape[1]), lambda i, j: (i, 0)), row,
                  pl.BlockSpec((1, 1, d), lambda i, j: (i // per_b, 0, 0)),
                  *[expert((d, f), el) for el in slots], *[expert((d, f), el) for el in slots],
                  *[expert((f, d), el) for el in slots],
                  const(sg), const(su), const(sd), vec, vec],
        out_specs=row,
        out_shape=jax.ShapeDtypeStruct((n, d), F32),
        scratch_shapes=[pltpu.VMEM((tm, d), F32)],
        compiler_params=_params(("parallel", "arbitrary")),
        name="moe",
    )(u, gate, x1, g2, *[wg] * eg, *[wu] * eg, *[wd] * eg, sg, su, sd, ln_g, ln_b)


def _block_diag_tiles(w):
    nb, bw, _ = w.shape
    per = MXU_TILE // bw
    w = w.reshape(nb // per, per, bw, bw)
    eye = jnp.eye(per, dtype=w.dtype)
    tiles = jnp.einsum("gpij,pq->gpiqj", w, eye)
    return tiles.reshape(nb // per, MXU_TILE, MXU_TILE)


def kernel(x, c, w_ada, b_ada, w_in, conv_w, conv_b, lru_w_a, lru_b_a, lru_w_i, lru_b_i, lru_lambda,
           lam_q1, lam_k1, lam_q2, lam_k2, subln_g, p_rnn, p_attn, w_out, ln1_g, ln1_b,
           w_router, router_bias, we_gate, we_up, we_down, ws_gate, ws_up, ws_down, ln2_g, ln2_b):
    bsz, seq, d = x.shape
    n = bsz * seq
    x2 = x.reshape(n, d)
    l = 0

    ada, w_in_bf, p_rnn_bf, p_attn_bf, w_out_bf, sg_bf, su_bf, sd_bf = _ada(
        c, w_ada[l], b_ada[l], [w_in[l], p_rnn[l], p_attn[l], w_out[l], ws_gate[l], ws_up[l], ws_down[l]])
    sh1, sc1, g1, sh2, sc2, g2 = [ada[:, i * d:(i + 1) * d].reshape(bsz, 1, d) for i in range(6)]

    proj = _inproj(x2, sh1, sc1, w_in_bf, seq, tm=TILES["inproj_rows"])

    ma = _rnn(proj, conv_w[l], conv_b[l].reshape(1, d),
              _block_diag_tiles(lru_w_a[l]).astype(BF16), lru_b_a[l].reshape(1, d),
              _block_diag_tiles(lru_w_i[l]).astype(BF16), lru_b_i[l].reshape(1, d),
              lru_lambda[l].reshape(1, d), p_rnn_bf, bsz, seq, tt=TILES["rnn_rows"])

    yb, wg_bf, wu_bf, wd_bf = _attn(proj, lam_q1[l].reshape(1, -1), lam_k1[l].reshape(1, -1),
                                    lam_q2[l].reshape(1, -1), lam_k2[l].reshape(1, -1),
                                    subln_g[l].reshape(1, -1), [we_gate[l], we_up[l], we_down[l]],
                                    bsz, seq, tq=TILES["attn_q_rows"])

    wr_hi = w_router[l].astype(BF16)
    wr_lo = (w_router[l] - wr_hi.astype(F32)).astype(BF16)
    x1, u, logits = _postmix(yb, ma, proj, x2, g1, sh2, sc2, ln1_g[l].reshape(1, d),
                             ln1_b[l].reshape(1, d), p_attn_bf,
                             w_out_bf, jnp.concatenate([wr_hi, wr_lo], axis=1), seq,
                             tm=TILES["postmix_rows"])

    gate = _route(logits, router_bias[l], tn=TILES["route_tokens"])

    out = _moe(u, gate, x1, g2, wg_bf, wu_bf, wd_bf,
               sg_bf, su_bf, sd_bf,
               ln2_g[l].reshape(1, d), ln2_b[l].reshape(1, d),
               seq, tm=TILES["moe_rows"], eg=TILES["moe_experts"])
    return out.reshape(bsz, seq, d)
```

```python
import functools
import math

import jax
import jax.numpy as jnp
from jax import lax
from jax.experimental import pallas as pl
from jax.experimental.pallas import tpu as pltpu

F32 = jnp.float32
BF16 = jnp.bfloat16

D_MODEL = 1024
CHUNK = 64
CONV_W = 4
LRU_C = 8.0
N_HEADS = 8
DIFF_DH = 64
HEAD_W = 2 * DIFF_DH
N_EXPERTS = 64
TOP_K = 8
N_GROUPS = 8
GROUP_SIZE = N_EXPERTS // N_GROUPS
TOPK_GROUPS = 4
ROUTED_SCALE = 2.5
DEPTH = 1
ALPHA = (2.0 * DEPTH) ** 0.25
LN_EPS = 1e-5
RMS_EPS = 1e-5
LAM_INIT = 0.8 - 0.6 * math.exp(-0.3 * 0)

COL_X, COL_Y, COL_Q, COL_K, COL_V, COL_GR, COL_GA = range(7)
N_COLS = 7

V7X_VMEM_LIMIT = 56 * 1024 * 1024
MXU_TILE = 256
SUBLANES = 8
LANES = 128
ADA_STEPS = 8
POSTMIX_CHUNK = 512

TILES = dict(
    inproj_rows=512,
    rnn_rows=512,
    attn_q_rows=256,
    postmix_rows=1024,
    route_tokens=1024,
    moe_rows=1024,
    moe_experts=5,
)


def _ln0(x):
    mu = jnp.mean(x, axis=-1, keepdims=True)
    xc = x - mu
    var = jnp.mean(xc * xc, axis=-1, keepdims=True)
    return xc * lax.rsqrt(var + LN_EPS)


def _silu(x):
    return x * jax.nn.sigmoid(x)


def _params(sem):
    return pltpu.CompilerParams(dimension_semantics=sem, vmem_limit_bytes=V7X_VMEM_LIMIT)


def _ada_kernel(c_ref, w_ref, b_ref, *rest):
    n_cast = len(rest) // 2
    o_ref = rest[n_cast]
    for src, dst in zip(rest[:n_cast], rest[n_cast + 1:]):
        dst[...] = src[...].astype(BF16)
    cond = _silu(c_ref[...])
    o_ref[...] = jnp.dot(cond, w_ref[...], preferred_element_type=F32,
                         precision=lax.Precision.HIGHEST) + b_ref[...]


def _ada(c, w, b, to_bf16):
    bsz, d = c.shape
    n_out = w.shape[1]
    steps = ADA_STEPS
    tn = n_out // steps

    def slab(a):
        assert a.shape[0] % (steps * 2 * SUBLANES) == 0
        return pl.BlockSpec((a.shape[0] // steps, a.shape[1]), lambda j: (j, 0))

    return pl.pallas_call(
        _ada_kernel,
        grid=(steps,),
        in_specs=[pl.BlockSpec((bsz, d), lambda j: (0, 0)),
                  pl.BlockSpec((d, tn), lambda j: (0, j)),
                  pl.BlockSpec((1, tn), lambda j: (0, j)),
                  *[slab(a) for a in to_bf16]],
        out_specs=[pl.BlockSpec((bsz, tn), lambda j: (0, j)), *[slab(a) for a in to_bf16]],
        out_shape=[jax.ShapeDtypeStruct((bsz, n_out), F32),
                   *[jax.ShapeDtypeStruct(a.shape, BF16) for a in to_bf16]],
        compiler_params=_params(("parallel",)),
        name="ada",
    )(c, w, b.reshape(1, n_out), *to_bf16)


def _inproj_kernel(x_ref, sh_ref, sc_ref, w_ref, o_ref):
    d = x_ref.shape[1]
    h = (_ln0(x_ref[...]) * (1.0 + sc_ref[0]) + sh_ref[0]).astype(BF16)
    for j in range(N_COLS):
        cols = slice(j * d, (j + 1) * d)
        acc = jnp.dot(h, w_ref[:, cols], preferred_element_type=F32)
        if j >= COL_GR:
            acc = jax.nn.sigmoid(acc)
        o_ref[:, cols] = acc.astype(BF16)


def _inproj(x2, sh, sc, w, seq, tm):
    n, d = x2.shape
    per_b = seq // tm
    return pl.pallas_call(
        _inproj_kernel,
        grid=(n // tm,),
        in_specs=[pl.BlockSpec((tm, d), lambda i: (i, 0)),
                  pl.BlockSpec((1, 1, d), lambda i: (i // per_b, 0, 0)),
                  pl.BlockSpec((1, 1, d), lambda i: (i // per_b, 0, 0)),
                  pl.BlockSpec(w.shape, lambda i: (0, 0))],
        out_specs=pl.BlockSpec((tm, N_COLS * d), lambda i: (i, 0)),
        out_shape=jax.ShapeDtypeStruct((n, N_COLS * d), BF16),
        compiler_params=_params(("parallel",)),
        name="inproj",
    )(x2, sh, sc, w)


def _stream_pitch(n_slab):
    assert n_slab % SUBLANES == 0
    return n_slab if (n_slab // SUBLANES) % 2 else n_slab + SUBLANES


def _rnn_kernel(xr_ref, yr_ref, g_ref, cw_ref, cb_ref, wa_ref, ba_ref, wi_ref, bi_ref,
                lam_ref, p_ref, o_ref, x_scr, h_stage, tail_scr, h_scr):
    t = pl.program_id(1)
    tt, d = xr_ref.shape
    n_slab = tt // SUBLANES
    n_hist = CONV_W - 1

    @pl.when(t == 0)
    def _():
        tail_scr[...] = jnp.zeros_like(tail_scr)
        h_scr[...] = jnp.zeros_like(h_scr)

    def slab(v, j):
        return v[j * SUBLANES:(j + 1) * SUBLANES, :]

    n_lane = d // LANES
    pitch = _stream_pitch(n_slab)

    def permuted(src_ref, scr):
        x = src_ref[...].astype(F32)
        for c in range(n_lane):
            for s in range(SUBLANES):
                scr[c, s * pitch:s * pitch + n_slab, :] = x[s * n_slab:(s + 1) * n_slab, c * LANES:(c + 1) * LANES]
        return [jnp.concatenate([scr[c, pl.ds(j, SUBLANES, stride=pitch), :] for c in range(n_lane)],
                                axis=-1) for j in range(n_slab)]

    sub = lax.broadcasted_iota(jnp.int32, (SUBLANES, d), 0)

    xs = permuted(xr_ref, x_scr)
    wrapped = []
    for i in range(n_hist):
        cur = xs[n_slab - n_hist + i]
        prev = slab(tail_scr, i)
        wrapped.append(pltpu.roll(jnp.where(sub == SUBLANES - 1, prev, cur), 1, 0))
        tail_scr[i * SUBLANES:(i + 1) * SUBLANES, :] = cur

    def x_at(j):
        return xs[j] if j >= 0 else wrapped[j + n_hist]

    cw = cw_ref[...]
    w = [cw[i:i + 1, :] for i in range(CONV_W)]
    cb = cb_ref[...]
    xc = jnp.concatenate(
        [sum((w[CONV_W - 1 - k] * x_at(j - k) for k in range(CONV_W)), start=cb) for j in range(n_slab)],
        axis=0)

    xcb = xc.astype(BF16)
    n_g = d // MXU_TILE
    ra = jnp.concatenate(
        [jnp.dot(xcb[:, g * MXU_TILE:(g + 1) * MXU_TILE], wa_ref[g], preferred_element_type=F32)
         for g in range(n_g)], axis=-1)
    ri = jnp.concatenate(
        [jnp.dot(xcb[:, g * MXU_TILE:(g + 1) * MXU_TILE], wi_ref[g], preferred_element_type=F32)
         for g in range(n_g)], axis=-1)
    r = jax.nn.sigmoid(ra + ba_ref[...])
    gi = jax.nn.sigmoid(ri + bi_ref[...])
    z = -lam_ref[...]
    softplus = jnp.maximum(z, 0.0) + jnp.log1p(jnp.exp(-jnp.abs(z)))
    log_a = (-LRU_C) * r * softplus
    a = jnp.exp(log_a)
    mult = jnp.sqrt(jnp.maximum(1.0 - a * a, 0.0))
    u = xc * gi * mult

    h_loc, p_loc = [slab(u, 0)], [slab(a, 0)]
    for j in range(1, n_slab):
        aj = slab(a, j)
        h_loc.append(aj * h_loc[-1] + slab(u, j))
        p_loc.append(aj * p_loc[-1])
    pa, hb = p_loc[-1], h_loc[-1]
    step = 1
    while step < SUBLANES:
        valid = sub >= step
        hb = jnp.where(valid, pa * pltpu.roll(hb, step, 0) + hb, hb)
        pa = jnp.where(valid, pa * pltpu.roll(pa, step, 0), pa)
        step *= 2
    carry = h_scr[...]
    end = pa * carry + hb
    start = jnp.where(sub == 0, carry, pltpu.roll(end, 1, 0))
    h_scr[...] = end[SUBLANES - 1:SUBLANES, :]
    for j in range(n_slab):
        hj = h_loc[j] + p_loc[j] * start
        for c in range(n_lane):
            h_stage[c, pl.ds(j, SUBLANES, stride=pitch), :] = hj[:, c * LANES:(c + 1) * LANES]
    h = jnp.concatenate(
        [jnp.concatenate([h_stage[c, s * pitch:s * pitch + n_slab, :] for s in range(SUBLANES)], axis=0)
         for c in range(n_lane)], axis=-1)
    ya = (h * jax.nn.gelu(yr_ref[...].astype(F32))).astype(BF16)
    out = g_ref[...].astype(F32) * jnp.dot(ya, p_ref[...], preferred_element_type=F32)
    o_ref[...] = out.astype(BF16)


def _rnn(proj, cw, cb, wa, ba, wi, bi, lam, p_rnn, bsz, seq, tt):
    n = proj.shape[0]
    d = D_MODEL
    per_b = seq // tt
    row = lambda b, t: b * per_b + t
    vec = pl.BlockSpec((1, d), lambda b, t: (0, 0))
    gate_w = pl.BlockSpec(wa.shape, lambda b, t: (0, 0, 0))
    tile = pltpu.VMEM((d // LANES, SUBLANES * _stream_pitch(tt // SUBLANES), LANES), F32)
    return pl.pallas_call(
        _rnn_kernel,
        grid=(bsz, per_b),
        in_specs=[pl.BlockSpec((tt, d), lambda b, t: (row(b, t), COL_X)),
                  pl.BlockSpec((tt, d), lambda b, t: (row(b, t), COL_Y)),
                  pl.BlockSpec((tt, d), lambda b, t: (row(b, t), COL_GR)),
                  pl.BlockSpec((CONV_W, d), lambda b, t: (0, 0)),
                  vec, gate_w, vec, gate_w, vec, vec,
                  pl.BlockSpec((d, d), lambda b, t: (0, 0))],
        out_specs=pl.BlockSpec((tt, d), lambda b, t: (row(b, t), 0)),
        out_shape=jax.ShapeDtypeStruct((n, d), BF16),
        scratch_shapes=[tile, tile,
                        pltpu.VMEM(((CONV_W - 1) * SUBLANES, d), F32), pltpu.VMEM((1, d), F32)],
        compiler_params=_params(("parallel", "arbitrary")),
        name="rnn",
    )(proj, proj, proj, cw, cb, wa, ba, wi, bi, lam, p_rnn)


def _attn_kernel(q_ref, k_ref, v_ref, lq1_ref, lk1_ref, lq2_ref, lk2_ref, sg_ref, *rest, tq):
    n_cast = len(rest) // 2
    o_ref = rest[n_cast]
    for src, dst in zip(rest[:n_cast], rest[n_cast + 1:]):
        dst[...] = src[...].astype(BF16)
    seq = k_ref.shape[0]
    lam = (jnp.exp(jnp.sum(lq1_ref[...] * lk1_ref[...], axis=-1, keepdims=True))
           - jnp.exp(jnp.sum(lq2_ref[...] * lk2_ref[...], axis=-1, keepdims=True)) + LAM_INIT)
    nt = (((1,), (1,)), ((), ()))
    scale = DIFF_DH ** -0.5
    lane = lax.broadcasted_iota(jnp.int32, (tq, HEAD_W), 1)
    diag = (lax.broadcasted_iota(jnp.int32, (tq, tq), 1) // CHUNK
            <= lax.broadcasted_iota(jnp.int32, (tq, tq), 0) // CHUNK)

    diag2 = jnp.concatenate([diag, diag], axis=0)

    for qi in reversed(range(seq // tq)):
        kv = (qi + 1) * tq
        q = q_ref[qi * tq:(qi + 1) * tq, :] * scale
        zero = jnp.zeros_like(q)
        qq = jnp.concatenate([jnp.where(lane < DIFF_DH, q, zero), jnp.where(lane >= DIFF_DH, q, zero)], axis=0)
        s = lax.dot_general(qq, k_ref[0:kv, :], nt, preferred_element_type=F32)
        s_diag = jnp.where(diag2, s[:, kv - tq:], -jnp.inf)
        s = s_diag if kv == tq else jnp.concatenate([s[:, :kv - tq], s_diag], axis=1)
        e = jnp.exp(s - jnp.max(s, axis=-1, keepdims=True))
        pv = jnp.dot(e.astype(BF16), v_ref[0:kv, :], preferred_element_type=F32)
        pv = pv / jnp.sum(e, axis=-1, keepdims=True)
        o = pv[:tq] - lam * pv[tq:]
        o = o * lax.rsqrt(jnp.mean(o * o, axis=-1, keepdims=True) + RMS_EPS) * sg_ref[...]
        o_ref[qi * tq:(qi + 1) * tq, :] = (o * (1.0 - LAM_INIT)).astype(BF16)


def _attn(proj, lq1, lk1, lq2, lk2, sg, to_bf16, bsz, seq, tq):
    n = proj.shape[0]
    steps = bsz * N_HEADS

    def slab(a):
        assert a.shape[0] % steps == 0
        return pl.BlockSpec((a.shape[0] // steps,) + a.shape[1:], lambda b, h: (b * N_HEADS + h, 0, 0))

    hb = D_MODEL // HEAD_W
    lam_spec = pl.BlockSpec((1, DIFF_DH), lambda b, h: (0, 0))
    return pl.pallas_call(
        functools.partial(_attn_kernel, tq=tq),
        grid=(bsz, N_HEADS),
        in_specs=[pl.BlockSpec((seq, HEAD_W), lambda b, h: (b, COL_Q * hb + h)),
                  pl.BlockSpec((seq, HEAD_W), lambda b, h: (b, COL_K * hb + h)),
                  pl.BlockSpec((seq, HEAD_W), lambda b, h: (b, COL_V * hb + h)),
                  lam_spec, lam_spec, lam_spec, lam_spec,
                  pl.BlockSpec((1, HEAD_W), lambda b, h: (0, 0)),
                  *[slab(a) for a in to_bf16]],
        out_specs=[pl.BlockSpec((seq, HEAD_W), lambda b, h: (b, h)), *[slab(a) for a in to_bf16]],
        out_shape=[jax.ShapeDtypeStruct((n, D_MODEL), BF16),
                   *[jax.ShapeDtypeStruct(a.shape, BF16) for a in to_bf16]],
        compiler_params=_params(("parallel", "parallel")),
        name="attn",
    )(proj, proj, proj, lq1, lk1, lq2, lk2, sg, *to_bf16)


def _postmix_kernel(yb_ref, ma_ref, ga_ref, x_ref, g1_ref, sh_ref, sc_ref, lg_ref, lb_ref,
                    pa_ref, wo_ref, wr_ref, x1_ref, u_ref, lt_ref):
    tm = x_ref.shape[0]

    def project(rows):
        branch_b = jnp.dot(yb_ref[rows, :], pa_ref[...], preferred_element_type=F32)
        merged = ma_ref[rows, :].astype(F32) + ga_ref[rows, :].astype(F32) * branch_b
        return jnp.dot(merged.astype(BF16), wo_ref[...], preferred_element_type=F32)

    def normalize(rows, mix):
        x1 = _ln0(ALPHA * x_ref[rows, :] + g1_ref[0] * mix) * lg_ref[...] + lb_ref[...]
        x1_ref[rows, :] = x1
        u = _ln0(x1) * (1.0 + sc_ref[0]) + sh_ref[0]
        u_ref[rows, :] = u.astype(BF16)
        u_hi = u.astype(BF16)
        u_lo = (u - u_hi.astype(F32)).astype(BF16)
        a = jnp.dot(u_hi, wr_ref[...], preferred_element_type=F32)
        b = jnp.dot(u_lo, wr_ref[...], preferred_element_type=F32)
        lt_ref[rows, :] = a + pltpu.roll(a, N_EXPERTS, 1) + b

    chunks = [slice(r0, r0 + POSTMIX_CHUNK) for r0 in range(0, tm, POSTMIX_CHUNK)]
    mix = project(chunks[0])
    for prev, rows in zip(chunks[:-1], chunks[1:]):
        nxt = project(rows)
        normalize(prev, mix)
        mix = nxt
    normalize(chunks[-1], mix)


def _postmix(yb, ma, proj, x2, g1, sh2, sc2, ln_g, ln_b, p_attn, w_out, wr_hilo, seq, tm):
    n, d = x2.shape
    per_b = seq // tm
    row = pl.BlockSpec((tm, d), lambda i: (i, 0))
    mod = pl.BlockSpec((1, 1, d), lambda i: (i // per_b, 0, 0))
    vec = pl.BlockSpec((1, d), lambda i: (0, 0))
    mat = pl.BlockSpec((d, d), lambda i: (0, 0))
    return pl.pallas_call(
        _postmix_kernel,
        grid=(n // tm,),
        in_specs=[row, row, pl.BlockSpec((tm, d), lambda i: (i, COL_GA)), row,
                  mod, mod, mod, vec, vec, mat, mat,
                  pl.BlockSpec((d, 2 * N_EXPERTS), lambda i: (0, 0))],
        out_specs=[row, row, pl.BlockSpec((tm, 2 * N_EXPERTS), lambda i: (i, 0))],
        out_shape=[jax.ShapeDtypeStruct((n, d), F32),
                   jax.ShapeDtypeStruct((n, d), BF16),
                   jax.ShapeDtypeStruct((n, 2 * N_EXPERTS), F32)],
        compiler_params=_params(("parallel",)),
        name="postmix",
    )(yb, ma, proj, x2, g1, sh2, sc2, ln_g, ln_b, p_attn, w_out, wr_hilo)


def _route_kernel(lt_ref, bias_ref, gate_ref):
    tn = lt_ref.shape[0]
    shape3 = (N_GROUPS, GROUP_SIZE, tn)
    neg = -jnp.inf
    s = jax.nn.sigmoid(lt_ref[...].T[:N_EXPERTS, :])
    sel = (s + bias_ref[...]).reshape(shape3)
    s = s.reshape(shape3)
    j_idx = lax.broadcasted_iota(jnp.int32, shape3, 1).astype(F32)
    g_idx = lax.broadcasted_iota(jnp.int32, shape3, 0).astype(F32)
    e_idx = g_idx * GROUP_SIZE + j_idx

    m1 = jnp.max(sel, axis=1, keepdims=True)
    first = jnp.min(jnp.where(sel == m1, j_idx, float(GROUP_SIZE)), axis=1, keepdims=True)
    m2 = jnp.max(jnp.where(j_idx == first, neg, sel), axis=1, keepdims=True)
    gscore = m1 + m2

    gi = lax.broadcasted_iota(jnp.int32, gscore.shape, 0).astype(F32)
    gkeep = jnp.zeros(gscore.shape, jnp.bool_)
    for _ in range(TOPK_GROUPS):
        m = jnp.max(gscore, axis=0, keepdims=True)
        pick = jnp.min(jnp.where(gscore == m, gi, float(N_GROUPS)), axis=0, keepdims=True)
        hit = gi == pick
        gkeep = jnp.logical_or(gkeep, hit)
        gscore = jnp.where(hit, neg, gscore)
    sel = jnp.where(gkeep, sel, neg)

    picked = jnp.zeros(shape3, F32)
    for _ in range(TOP_K):
        m = jnp.max(jnp.max(sel, axis=1, keepdims=True), axis=0, keepdims=True)
        cand = jnp.where(sel == m, e_idx, float(N_EXPERTS))
        pick = jnp.min(jnp.min(cand, axis=1, keepdims=True), axis=0, keepdims=True)
        hit = e_idx == pick
        picked = jnp.where(hit, s, picked)
        sel = jnp.where(hit, neg, sel)
    total = jnp.sum(jnp.sum(picked, axis=1, keepdims=True), axis=0, keepdims=True)
    gate = picked / total * ROUTED_SCALE
    gate = gate.reshape(N_EXPERTS, tn)
    shared = (lax.broadcasted_iota(jnp.int32, gate.shape, 0) == 0).astype(F32)
    gate_ref[...] = jnp.concatenate([gate, shared], axis=0).T


def _route(logits, bias, tn):
    n, w = logits.shape
    e = bias.shape[0]
    return pl.pallas_call(
        _route_kernel,
        grid=(n // tn,),
        in_specs=[pl.BlockSpec((tn, w), lambda i: (i, 0)),
                  pl.BlockSpec((e, 1), lambda i: (0, 0))],
        out_specs=pl.BlockSpec((tn, w), lambda i: (i, 0)),
        out_shape=jax.ShapeDtypeStruct((n, w), F32),
        compiler_params=_params(("parallel",)),
        name="route",
    )(logits, bias.reshape(e, 1))


def _moe_kernel(*refs, eg):
    u_ref, gate_ref, x1_ref, g2_ref = refs[:4]
    wg_refs, wu_refs, wd_refs = refs[4:4 + eg], refs[4 + eg:4 + 2 * eg], refs[4 + 2 * eg:4 + 3 * eg]
    sg_ref, su_ref, sd_ref, lg_ref, lb_ref, o_ref, acc_scr = refs[4 + 3 * eg:]
    j = pl.program_id(1)
    last = j == pl.num_programs(1) - 1

    @pl.when(j == 0)
    def _():
        acc_scr[...] = jnp.zeros_like(acc_scr)

    u = u_ref[...]
    gate = gate_ref[...]
    lane = lax.broadcasted_iota(jnp.int32, gate.shape, 1)
    acts, downs = [], []
    for el in range(eg):
        wg, wu, wd = wg_refs[el][0], wu_refs[el][0], wd_refs[el][0]
        if el == eg - 1:
            wg = jnp.where(last, sg_ref[...], wg)
            wu = jnp.where(last, su_ref[...], wu)
            wd = jnp.where(last, sd_ref[...], wd)
        gk = jnp.sum(jnp.where(lane == j * eg + el, gate, 0.0), axis=-1, keepdims=True)
        hg = jnp.dot(u, wg, preferred_element_type=F32)
        hu = jnp.dot(u, wu, preferred_element_type=F32)
        acts.append((_silu(hg) * hu * gk).astype(BF16))
        downs.append(wd)
    acc_scr[...] += jnp.dot(jnp.concatenate(acts, axis=-1), jnp.concatenate(downs, axis=0),
                            preferred_element_type=F32)

    @pl.when(last)
    def _():
        z = ALPHA * x1_ref[...] + g2_ref[0] * acc_scr[...]
        o_ref[...] = _ln0(z) * lg_ref[...] + lb_ref[...]


def _moe(u, gate, x1, g2, wg, wu, wd, sg, su, sd, ln_g, ln_b, seq, tm, eg):
    n, d = x1.shape
    n_e, _, f = wg.shape
    assert (n_e + 1) % eg == 0 and n_e < gate.shape[1] and sg.shape == (d, f)
    per_b = seq // tm
    row = pl.BlockSpec((tm, d), lambda i, j: (i, 0))
    vec = pl.BlockSpec((1, d), lambda i, j: (0, 0))
    const = lambda a: pl.BlockSpec(a.shape, lambda i, j: (0, 0))

    def expert(shape, el):
        return pl.BlockSpec((1,) + shape, lambda i, j: (jnp.minimum(j * eg + el, n_e - 1), 0, 0))

    slots = range(eg)
    return pl.pallas_call(
        functools.partial(_moe_kernel, eg=eg),
        grid=(n // tm, (n_e + 1) // eg),
        in_specs=[row, pl.BlockSpec((tm, gate.shape[1]), lambda i, j: (i, 0)), row,
                  pl.BlockSpec((1, 1, d), lambda i, j: (i // per_b, 0, 0)),
                  *[expert((d, f), el) for el in slots], *[expert((d, f), el) for el in slots],
                  *[expert((f, d), el) for el in slots],
                  const(sg), const(su), const(sd), vec, vec],
        out_specs=row,
        out_shape=jax.ShapeDtypeStruct((n, d), F32),
        scratch_shapes=[pltpu.VMEM((tm, d), F32)],
        compiler_params=_params(("parallel", "arbitrary")),
        name="moe",
    )(u, gate, x1, g2, *[wg] * eg, *[wu] * eg, *[wd] * eg, sg, su, sd, ln_g, ln_b)


def _block_diag_tiles(w):
    nb, bw, _ = w.shape
    per = MXU_TILE // bw
    w = w.reshape(nb // per, per, bw, bw)
    eye = jnp.eye(per, dtype=w.dtype)
    tiles = jnp.einsum("gpij,pq->gpiqj", w, eye)
    return tiles.reshape(nb // per, MXU_TILE, MXU_TILE)


def kernel(x, c, w_ada, b_ada, w_in, conv_w, conv_b, lru_w_a, lru_b_a, lru_w_i, lru_b_i, lru_lambda,
           lam_q1, lam_k1, lam_q2, lam_k2, subln_g, p_rnn, p_attn, w_out, ln1_g, ln1_b,
           w_router, router_bias, we_gate, we_up, we_down, ws_gate, ws_up, ws_down, ln2_g, ln2_b):
    bsz, seq, d = x.shape
    n = bsz * seq
    x2 = x.reshape(n, d)
    l = 0

    ada, w_in_bf, p_rnn_bf, p_attn_bf, w_out_bf, sg_bf, su_bf, sd_bf = _ada(
        c, w_ada[l], b_ada[l], [w_in[l], p_rnn[l], p_attn[l], w_out[l], ws_gate[l], ws_up[l], ws_down[l]])
    sh1, sc1, g1, sh2, sc2, g2 = [ada[:, i * d:(i + 1) * d].reshape(bsz, 1, d) for i in range(6)]

    proj = _inproj(x2, sh1, sc1, w_in_bf, seq, tm=TILES["inproj_rows"])

    ma = _rnn(proj, conv_w[l], conv_b[l].reshape(1, d),
              _block_diag_tiles(lru_w_a[l]).astype(BF16), lru_b_a[l].reshape(1, d),
              _block_diag_tiles(lru_w_i[l]).astype(BF16), lru_b_i[l].reshape(1, d),
              lru_lambda[l].reshape(1, d), p_rnn_bf, bsz, seq, tt=TILES["rnn_rows"])

    yb, wg_bf, wu_bf, wd_bf = _attn(proj, lam_q1[l].reshape(1, -1), lam_k1[l].reshape(1, -1),
                                    lam_q2[l].reshape(1, -1), lam_k2[l].reshape(1, -1),
                                    subln_g[l].reshape(1, -1), [we_gate[l], we_up[l], we_down[l]],
                                    bsz, seq, tq=TILES["attn_q_rows"])

    wr_hi = w_router[l].astype(BF16)
    wr_lo = (w_router[l] - wr_hi.astype(F32)).astype(BF16)
    x1, u, logits = _postmix(yb, ma, proj, x2, g1, sh2, sc2, ln1_g[l].reshape(1, d),
                             ln1_b[l].reshape(1, d), p_attn_bf,
                             w_out_bf, jnp.concatenate([wr_hi, wr_lo], axis=1), seq,
                             tm=TILES["postmix_rows"])

    gate = _route(logits, router_bias[l], tn=TILES["route_tokens"])

    out = _moe(u, gate, x1, g2, wg_bf, wu_bf, wd_bf,
               sg_bf, su_bf, sd_bf,
               ln2_g[l].reshape(1, d), ln2_b[l].reshape(1, d),
               seq, tm=TILES["moe_rows"], eg=TILES["moe_experts"])
    return out.reshape(bsz, seq, d)
```

```python
import functools
import math

import jax
import jax.numpy as jnp
from jax import lax
from jax.experimental import pallas as pl
from jax.experimental.pallas import tpu as pltpu

F32 = jnp.float32
BF16 = jnp.bfloat16

D_MODEL = 1024
CHUNK = 64
CONV_W = 4
LRU_C = 8.0
N_HEADS = 8
DIFF_DH = 64
HEAD_W = 2 * DIFF_DH
N_EXPERTS = 64
TOP_K = 8
N_GROUPS = 8
GROUP_SIZE = N_EXPERTS // N_GROUPS
TOPK_GROUPS = 4
ROUTED_SCALE = 2.5
DEPTH = 1
ALPHA = (2.0 * DEPTH) ** 0.25
LN_EPS = 1e-5
RMS_EPS = 1e-5
LAM_INIT = 0.8 - 0.6 * math.exp(-0.3 * 0)

COL_X, COL_Y, COL_Q, COL_K, COL_V, COL_GR, COL_GA = range(7)
N_COLS = 7

V7X_VMEM_LIMIT = 56 * 1024 * 1024
MXU_TILE = 256
SUBLANES = 8
LANES = 128
ADA_STEPS = 8
POSTMIX_CHUNK = 512

TILES = dict(
    inproj_rows=1024,
    rnn_rows=1024,
    attn_q_rows=256,
    postmix_rows=1024,
    route_tokens=2048,
    moe_rows=1024,
    moe_experts=5,
)


def _ln0(x):
    mu = jnp.mean(x, axis=-1, keepdims=True)
    xc = x - mu
    var = jnp.mean(xc * xc, axis=-1, keepdims=True)
    return xc * lax.rsqrt(var + LN_EPS)


def _silu(x):
    return x * jax.nn.sigmoid(x)


def _params(sem):
    return pltpu.CompilerParams(dimension_semantics=sem, vmem_limit_bytes=V7X_VMEM_LIMIT)


def _ada_kernel(c_ref, w_ref, b_ref, *rest):
    n_cast = len(rest) // 2
    o_ref = rest[n_cast]
    for src, dst in zip(rest[:n_cast], rest[n_cast + 1:]):
        dst[...] = src[...].astype(BF16)
    cond = _silu(c_ref[...])
    o_ref[...] = jnp.dot(cond, w_ref[...], preferred_element_type=F32,
                         precision=lax.Precision.HIGHEST) + b_ref[...]


def _ada(c, w, b, to_bf16):
    bsz, d = c.shape
    n_out = w.shape[1]
    steps = ADA_STEPS
    tn = n_out // steps

    def slab(a):
        assert a.shape[0] % (steps * 2 * SUBLANES) == 0
        return pl.BlockSpec((a.shape[0] // steps, a.shape[1]), lambda j: (j, 0))

    return pl.pallas_call(
        _ada_kernel,
        grid=(steps,),
        in_specs=[pl.BlockSpec((bsz, d), lambda j: (0, 0)),
                  pl.BlockSpec((d, tn), lambda j: (0, j)),
                  pl.BlockSpec((1, tn), lambda j: (0, j)),
                  *[slab(a) for a in to_bf16]],
        out_specs=[pl.BlockSpec((bsz, tn), lambda j: (0, j)), *[slab(a) for a in to_bf16]],
        out_shape=[jax.ShapeDtypeStruct((bsz, n_out), F32),
                   *[jax.ShapeDtypeStruct(a.shape, BF16) for a in to_bf16]],
        compiler_params=_params(("parallel",)),
        name="ada",
    )(c, w, b.reshape(1, n_out), *to_bf16)


def _inproj_kernel(x_ref, sh_ref, sc_ref, w_ref, o_ref):
    d = x_ref.shape[1]
    h = (_ln0(x_ref[...]) * (1.0 + sc_ref[0]) + sh_ref[0]).astype(BF16)
    for j in range(N_COLS):
        cols = slice(j * d, (j + 1) * d)
        acc = jnp.dot(h, w_ref[:, cols], preferred_element_type=F32)
        if j >= COL_GR:
            acc = jax.nn.sigmoid(acc)
        o_ref[:, cols] = acc.astype(BF16)


def _inproj(x2, sh, sc, w, seq, tm):
    n, d = x2.shape
    per_b = seq // tm
    return pl.pallas_call(
        _inproj_kernel,
        grid=(n // tm,),
        in_specs=[pl.BlockSpec((tm, d), lambda i: (i, 0)),
                  pl.BlockSpec((1, 1, d), lambda i: (i // per_b, 0, 0)),
                  pl.BlockSpec((1, 1, d), lambda i: (i // per_b, 0, 0)),
                  pl.BlockSpec(w.shape, lambda i: (0, 0))],
        out_specs=pl.BlockSpec((tm, N_COLS * d), lambda i: (i, 0)),
        out_shape=jax.ShapeDtypeStruct((n, N_COLS * d), BF16),
        compiler_params=_params(("parallel",)),
        name="inproj",
    )(x2, sh, sc, w)


def _stream_pitch(n_slab):
    assert n_slab % SUBLANES == 0
    return n_slab if (n_slab // SUBLANES) % 2 else n_slab + SUBLANES


def _rnn_kernel(xr_ref, yr_ref, g_ref, cw_ref, cb_ref, wa_ref, ba_ref, wi_ref, bi_ref,
                lam_ref, p_ref, o_ref, x_scr, h_stage, tail_scr, h_scr):
    t = pl.program_id(1)
    tt, d = xr_ref.shape
    n_slab = tt // SUBLANES
    n_hist = CONV_W - 1

    @pl.when(t == 0)
    def _():
        tail_scr[...] = jnp.zeros_like(tail_scr)
        h_scr[...] = jnp.zeros_like(h_scr)

    def slab(v, j):
        return v[j * SUBLANES:(j + 1) * SUBLANES, :]

    n_lane = d // LANES
    pitch = _stream_pitch(n_slab)

    def permuted(src_ref, scr):
        x = src_ref[...].astype(F32)
        for c in range(n_lane):
            for s in range(SUBLANES):
                scr[c, s * pitch:s * pitch + n_slab, :] = x[s * n_slab:(s + 1) * n_slab, c * LANES:(c + 1) * LANES]
        return [jnp.concatenate([scr[c, pl.ds(j, SUBLANES, stride=pitch), :] for c in range(n_lane)],
                                axis=-1) for j in range(n_slab)]

    sub = lax.broadcasted_iota(jnp.int32, (SUBLANES, d), 0)

    xs = permuted(xr_ref, x_scr)
    wrapped = []
    for i in range(n_hist):
        cur = xs[n_slab - n_hist + i]
        prev = slab(tail_scr, i)
        wrapped.append(pltpu.roll(jnp.where(sub == SUBLANES - 1, prev, cur), 1, 0))
        tail_scr[i * SUBLANES:(i + 1) * SUBLANES, :] = cur

    def x_at(j):
        return xs[j] if j >= 0 else wrapped[j + n_hist]

    cw = cw_ref[...]
    w = [cw[i:i + 1, :] for i in range(CONV_W)]
    cb = cb_ref[...]
    xc = jnp.concatenate(
        [sum((w[CONV_W - 1 - k] * x_at(j - k) for k in range(CONV_W)), start=cb) for j in range(n_slab)],
        axis=0)

    xcb = xc.astype(BF16)
    n_g = d // MXU_TILE
    ra = jnp.concatenate(
        [jnp.dot(xcb[:, g * MXU_TILE:(g + 1) * MXU_TILE], wa_ref[g], preferred_element_type=F32)
         for g in range(n_g)], axis=-1)
    ri = jnp.concatenate(
        [jnp.dot(xcb[:, g * MXU_TILE:(g + 1) * MXU_TILE], wi_ref[g], preferred_element_type=F32)
         for g in range(n_g)], axis=-1)
    r = jax.nn.sigmoid(ra + ba_ref[...])
    gi = jax.nn.sigmoid(ri + bi_ref[...])
    z = -lam_ref[...]
    softplus = jnp.maximum(z, 0.0) + jnp.log1p(jnp.exp(-jnp.abs(z)))
    log_a = (-LRU_C) * r * softplus
    a = jnp.exp(log_a)
    mult = jnp.sqrt(jnp.maximum(1.0 - a * a, 0.0))
    u = xc * gi * mult

    h_loc, p_loc = [slab(u, 0)], [slab(a, 0)]
    for j in range(1, n_slab):
        aj = slab(a, j)
        h_loc.append(aj * h_loc[-1] + slab(u, j))
        p_loc.append(aj * p_loc[-1])
    pa, hb = p_loc[-1], h_loc[-1]
    step = 1
    while step < SUBLANES:
        valid = sub >= step
        hb = jnp.where(valid, pa * pltpu.roll(hb, step, 0) + hb, hb)
        pa = jnp.where(valid, pa * pltpu.roll(pa, step, 0), pa)
        step *= 2
    carry = h_scr[...]
    end = pa * carry + hb
    start = jnp.where(sub == 0, carry, pltpu.roll(end, 1, 0))
    h_scr[...] = end[SUBLANES - 1:SUBLANES, :]
    for j in range(n_slab):
        hj = h_loc[j] + p_loc[j] * start
        for c in range(n_lane):
            h_stage[c, pl.ds(j, SUBLANES, stride=pitch), :] = hj[:, c * LANES:(c + 1) * LANES]
    h = jnp.concatenate(
        [jnp.concatenate([h_stage[c, s * pitch:s * pitch + n_slab, :] for s in range(SUBLANES)], axis=0)
         for c in range(n_lane)], axis=-1)
    ya = (h * jax.nn.gelu(yr_ref[...].astype(F32))).astype(BF16)
    out = g_ref[...].astype(F32) * jnp.dot(ya, p_ref[...], preferred_element_type=F32)
    o_ref[...] = out.astype(BF16)


def _rnn(proj, cw, cb, wa, ba, wi, bi, lam, p_rnn, bsz, seq, tt):
    n = proj.shape[0]
    d = D_MODEL
    per_b = seq // tt
    row = lambda b, t: b * per_b + t
    vec = pl.BlockSpec((1, d), lambda b, t: (0, 0))
    gate_w = pl.BlockSpec(wa.shape, lambda b, t: (0, 0, 0))
    tile = pltpu.VMEM((d // LANES, SUBLANES * _stream_pitch(tt // SUBLANES), LANES), F32)
    return pl.pallas_call(
        _rnn_kernel,
        grid=(bsz, per_b),
        in_specs=[pl.BlockSpec((tt, d), lambda b, t: (row(b, t), COL_X)),
                  pl.BlockSpec((tt, d), lambda b, t: (row(b, t), COL_Y)),
                  pl.BlockSpec((tt, d), lambda b, t: (row(b, t), COL_GR)),
                  pl.BlockSpec((CONV_W, d), lambda b, t: (0, 0)),
                  vec, gate_w, vec, gate_w, vec, vec,
                  pl.BlockSpec((d, d), lambda b, t: (0, 0))],
        out_specs=pl.BlockSpec((tt, d), lambda b, t: (row(b, t), 0)),
        out_shape=jax.ShapeDtypeStruct((n, d), BF16),
        scratch_shapes=[tile, tile,
                        pltpu.VMEM(((CONV_W - 1) * SUBLANES, d), F32), pltpu.VMEM((1, d), F32)],
        compiler_params=_params(("parallel", "arbitrary")),
        name="rnn",
    )(proj, proj, proj, cw, cb, wa, ba, wi, bi, lam, p_rnn)


def _attn_kernel(q_ref, k_ref, v_ref, lq1_ref, lk1_ref, lq2_ref, lk2_ref, sg_ref, *rest, tq):
    n_cast = len(rest) // 2
    o_ref = rest[n_cast]
    for src, dst in zip(rest[:n_cast], rest[n_cast + 1:]):
        dst[...] = src[...].astype(BF16)
    seq = k_ref.shape[0]
    lam = (jnp.exp(jnp.sum(lq1_ref[...] * lk1_ref[...], axis=-1, keepdims=True))
           - jnp.exp(jnp.sum(lq2_ref[...] * lk2_ref[...], axis=-1, keepdims=True)) + LAM_INIT)
    nt = (((1,), (1,)), ((), ()))
    scale = DIFF_DH ** -0.5
    lane = lax.broadcasted_iota(jnp.int32, (tq, HEAD_W), 1)
    diag = (lax.broadcasted_iota(jnp.int32, (tq, tq), 1) // CHUNK
            <= lax.broadcasted_iota(jnp.int32, (tq, tq), 0) // CHUNK)

    diag2 = jnp.concatenate([diag, diag], axis=0)

    for qi in reversed(range(seq // tq)):
        kv = (qi + 1) * tq
        q = q_ref[qi * tq:(qi + 1) * tq, :] * scale
        zero = jnp.zeros_like(q)
        qq = jnp.concatenate([jnp.where(lane < DIFF_DH, q, zero), jnp.where(lane >= DIFF_DH, q, zero)], axis=0)
        s = lax.dot_general(qq, k_ref[0:kv, :], nt, preferred_element_type=F32)
        s_diag = jnp.where(diag2, s[:, kv - tq:], -jnp.inf)
        s = s_diag if kv == tq else jnp.concatenate([s[:, :kv - tq], s_diag], axis=1)
        e = jnp.exp(s - jnp.max(s, axis=-1, keepdims=True))
        pv = jnp.dot(e.astype(BF16), v_ref[0:kv, :], preferred_element_type=F32)
        pv = pv / jnp.sum(e, axis=-1, keepdims=True)
        o = pv[:tq] - lam * pv[tq:]
        o = o * lax.rsqrt(jnp.mean(o * o, axis=-1, keepdims=True) + RMS_EPS) * sg_ref[...]
        o_ref[qi * tq:(qi + 1) * tq, :] = (o * (1.0 - LAM_INIT)).astype(BF16)


def _attn(proj, lq1, lk1, lq2, lk2, sg, to_bf16, bsz, seq, tq):
    n = proj.shape[0]
    steps = bsz * N_HEADS

    def slab(a):
        assert a.shape[0] % steps == 0
        return pl.BlockSpec((a.shape[0] // steps,) + a.shape[1:], lambda b, h: (b * N_HEADS + h, 0, 0))

    hb = D_MODEL // HEAD_W
    lam_spec = pl.BlockSpec((1, DIFF_DH), lambda b, h: (0, 0))
    return pl.pallas_call(
        functools.partial(_attn_kernel, tq=tq),
        grid=(bsz, N_HEADS),
        in_specs=[pl.BlockSpec((seq, HEAD_W), lambda b, h: (b, COL_Q * hb + h)),
                  pl.BlockSpec((seq, HEAD_W), lambda b, h: (b, COL_K * hb + h)),
                  pl.BlockSpec((seq, HEAD_W), lambda b, h: (b, COL_V * hb + h)),
                  lam_spec, lam_spec, lam_spec, lam_spec,
                  pl.BlockSpec((1, HEAD_W), lambda b, h: (0, 0)),
                  *[slab(a) for a in to_bf16]],
        out_specs=[pl.BlockSpec((seq, HEAD_W), lambda b, h: (b, h)), *[slab(a) for a in to_bf16]],
        out_shape=[jax.ShapeDtypeStruct((n, D_MODEL), BF16),
                   *[jax.ShapeDtypeStruct(a.shape, BF16) for a in to_bf16]],
        compiler_params=_params(("parallel", "parallel")),
        name="attn",
    )(proj, proj, proj, lq1, lk1, lq2, lk2, sg, *to_bf16)


def _postmix_kernel(yb_ref, ma_ref, ga_ref, x_ref, g1_ref, sh_ref, sc_ref, lg_ref, lb_ref,
                    pa_ref, wo_ref, wr_ref, x1_ref, u_ref, lt_ref):
    tm = x_ref.shape[0]

    def project(rows):
        branch_b = jnp.dot(yb_ref[rows, :], pa_ref[...], preferred_element_type=F32)
        merged = ma_ref[rows, :].astype(F32) + ga_ref[rows, :].astype(F32) * branch_b
        return jnp.dot(merged.astype(BF16), wo_ref[...], preferred_element_type=F32)

    def normalize(rows, mix):
        x1 = _ln0(ALPHA * x_ref[rows, :] + g1_ref[0] * mix) * lg_ref[...] + lb_ref[...]
        x1_ref[rows, :] = x1
        u = _ln0(x1) * (1.0 + sc_ref[0]) + sh_ref[0]
        u_ref[rows, :] = u.astype(BF16)
        u_hi = u.astype(BF16)
        u_lo = (u - u_hi.astype(F32)).astype(BF16)
        a = jnp.dot(u_hi, wr_ref[...], preferred_element_type=F32)
        b = jnp.dot(u_lo, wr_ref[...], preferred_element_type=F32)
        lt_ref[rows, :] = a + pltpu.roll(a, N_EXPERTS, 1) + b

    chunks = [slice(r0, r0 + POSTMIX_CHUNK) for r0 in range(0, tm, POSTMIX_CHUNK)]
    mix = project(chunks[0])
    for prev, rows in zip(chunks[:-1], chunks[1:]):
        nxt = project(rows)
        normalize(prev, mix)
        mix = nxt
    normalize(chunks[-1], mix)


def _postmix(yb, ma, proj, x2, g1, sh2, sc2, ln_g, ln_b, p_attn, w_out, wr_hilo, seq, tm):
    n, d = x2.shape
    per_b = seq // tm
    row = pl.BlockSpec((tm, d), lambda i: (i, 0))
    mod = pl.BlockSpec((1, 1, d), lambda i: (i // per_b, 0, 0))
    vec = pl.BlockSpec((1, d), lambda i: (0, 0))
    mat = pl.BlockSpec((d, d), lambda i: (0, 0))
    return pl.pallas_call(
        _postmix_kernel,
        grid=(n // tm,),
        in_specs=[row, row, pl.BlockSpec((tm, d), lambda i: (i, COL_GA)), row,
                  mod, mod, mod, vec, vec, mat, mat,
                  pl.BlockSpec((d, 2 * N_EXPERTS), lambda i: (0, 0))],
        out_specs=[row, row, pl.BlockSpec((tm, 2 * N_EXPERTS), lambda i: (i, 0))],
        out_shape=[jax.ShapeDtypeStruct((n, d), F32),
                   jax.ShapeDtypeStruct((n, d), BF16),
                   jax.ShapeDtypeStruct((n, 2 * N_EXPERTS), F32)],
        compiler_params=_params(("parallel",)),
        name="postmix",
    )(yb, ma, proj, x2, g1, sh2, sc2, ln_g, ln_b, p_attn, w_out, wr_hilo)


def _route_kernel(lt_ref, bias_ref, gate_ref):
    tn = lt_ref.shape[0]
    shape3 = (N_GROUPS, GROUP_SIZE, tn)
    neg = -jnp.inf
    s = jax.nn.sigmoid(lt_ref[...].T[:N_EXPERTS, :])
    sel = (s + bias_ref[...]).reshape(shape3)
    s = s.reshape(shape3)
    j_idx = lax.broadcasted_iota(jnp.int32, shape3, 1).astype(F32)
    g_idx = lax.broadcasted_iota(jnp.int32, shape3, 0).astype(F32)
    e_idx = g_idx * GROUP_SIZE + j_idx

    m1 = jnp.max(sel, axis=1, keepdims=True)
    first = jnp.min(jnp.where(sel == m1, j_idx, float(GROUP_SIZE)), axis=1, keepdims=True)
    m2 = jnp.max(jnp.where(j_idx == first, neg, sel), axis=1, keepdims=True)
    gscore = m1 + m2

    gi = lax.broadcasted_iota(jnp.int32, gscore.shape, 0).astype(F32)
    gkeep = jnp.zeros(gscore.shape, jnp.bool_)
    for _ in range(TOPK_GROUPS):
        m = jnp.max(gscore, axis=0, keepdims=True)
        pick = jnp.min(jnp.where(gscore == m, gi, float(N_GROUPS)), axis=0, keepdims=True)
        hit = gi == pick
        gkeep = jnp.logical_or(gkeep, hit)
        gscore = jnp.where(hit, neg, gscore)
    sel = jnp.where(gkeep, sel, neg)

    picked = jnp.zeros(shape3, F32)
    for _ in range(TOP_K):
        m = jnp.max(jnp.max(sel, axis=1, keepdims=True), axis=0, keepdims=True)
        cand = jnp.where(sel == m, e_idx, float(N_EXPERTS))
        pick = jnp.min(jnp.min(cand, axis=1, keepdims=True), axis=0, keepdims=True)
        hit = e_idx == pick
        picked = jnp.where(hit, s, picked)
        sel = jnp.where(hit, neg, sel)
    total = jnp.sum(jnp.sum(picked, axis=1, keepdims=True), axis=0, keepdims=True)
    gate = picked / total * ROUTED_SCALE
    gate = gate.reshape(N_EXPERTS, tn)
    shared = (lax.broadcasted_iota(jnp.int32, gate.shape, 0) == 0).astype(F32)
    gate_ref[...] = jnp.concatenate([gate, shared], axis=0).T


def _route(logits, bias, tn):
    n, w = logits.shape
    e = bias.shape[0]
    return pl.pallas_call(
        _route_kernel,
        grid=(n // tn,),
        in_specs=[pl.BlockSpec((tn, w), lambda i: (i, 0)),
                  pl.BlockSpec((e, 1), lambda i: (0, 0))],
        out_specs=pl.BlockSpec((tn, w), lambda i: (i, 0)),
        out_shape=jax.ShapeDtypeStruct((n, w), F32),
        compiler_params=_params(("parallel",)),
        name="route",
    )(logits, bias.reshape(e, 1))


def _moe_kernel(*refs, eg):
    u_ref, gate_ref, x1_ref, g2_ref = refs[:4]
    wg_refs, wu_refs, wd_refs = refs[4:4 + eg], refs[4 + eg:4 + 2 * eg], refs[4 + 2 * eg:4 + 3 * eg]
    sg_ref, su_ref, sd_ref, lg_ref, lb_ref, o_ref, acc_scr = refs[4 + 3 * eg:]
    j = pl.program_id(1)
    last = j == pl.num_programs(1) - 1

    @pl.when(j == 0)
    def _():
        acc_scr[...] = jnp.zeros_like(acc_scr)

    u = u_ref[...]
    gate = gate_ref[...]
    lane = lax.broadcasted_iota(jnp.int32, gate.shape, 1)
    acts, downs = [], []
    for el in range(eg):
        wg, wu, wd = wg_refs[el][0], wu_refs[el][0], wd_refs[el][0]
        if el == eg - 1:
            wg = jnp.where(last, sg_ref[...], wg)
            wu = jnp.where(last, su_ref[...], wu)
            wd = jnp.where(last, sd_ref[...], wd)
        gk = jnp.sum(jnp.where(lane == j * eg + el, gate, 0.0), axis=-1, keepdims=True)
        hg = jnp.dot(u, wg, preferred_element_type=F32)
        hu = jnp.dot(u, wu, preferred_element_type=F32)
        acts.append((_silu(hg) * hu * gk).astype(BF16))
        downs.append(wd)
    acc_scr[...] += jnp.dot(jnp.concatenate(acts, axis=-1), jnp.concatenate(downs, axis=0),
                            preferred_element_type=F32)

    @pl.when(last)
    def _():
        z = ALPHA * x1_ref[...] + g2_ref[0] * acc_scr[...]
        o_ref[...] = _ln0(z) * lg_ref[...] + lb_ref[...]


def _moe(u, gate, x1, g2, wg, wu, wd, sg, su, sd, ln_g, ln_b, seq, tm, eg):
    n, d = x1.shape
    n_e, _, f = wg.shape
    assert (n_e + 1) % eg == 0 and n_e < gate.shape[1] and sg.shape == (d, f)
    per_b = seq // tm
    row = pl.BlockSpec((tm, d), lambda i, j: (i, 0))
    vec = pl.BlockSpec((1, d), lambda i, j: (0, 0))
    const = lambda a: pl.BlockSpec(a.shape, lambda i, j: (0, 0))

    def expert(shape, el):
        return pl.BlockSpec((1,) + shape, lambda i, j: (jnp.minimum(j * eg + el, n_e - 1), 0, 0))

    slots = range(eg)
    return pl.pallas_call(
        functools.partial(_moe_kernel, eg=eg),
        grid=(n // tm, (n_e + 1) // eg),
        in_specs=[row, pl.BlockSpec((tm, gate.shape[1]), lambda i, j: (i, 0)), row,
                  pl.BlockSpec((1, 1, d), lambda i, j: (i // per_b, 0, 0)),
                  *[expert((d, f), el) for el in slots], *[expert((d, f), el) for el in slots],
                  *[expert((f, d), el) for el in slots],
                  const(sg), const(su), const(sd), vec, vec],
        out_specs=row,
        out_shape=jax.ShapeDtypeStruct((n, d), F32),
        scratch_shapes=[pltpu.VMEM((tm, d), F32)],
        compiler_params=_params(("parallel", "arbitrary")),
        name="moe",
    )(u, gate, x1, g2, *[wg] * eg, *[wu] * eg, *[wd] * eg, sg, su, sd, ln_g, ln_b)


def _block_diag_tiles(w):
    nb, bw, _ = w.shape
    per = MXU_TILE // bw
    w = w.reshape(nb // per, per, bw, bw)
    eye = jnp.eye(per, dtype=w.dtype)
    tiles = jnp.einsum("gpij,pq->gpiqj", w, eye)
    return tiles.reshape(nb // per, MXU_TILE, MXU_TILE)


def kernel(x, c, w_ada, b_ada, w_in, conv_w, conv_b, lru_w_a, lru_b_a, lru_w_i, lru_b_i, lru_lambda,
           lam_q1, lam_k1, lam_q2, lam_k2, subln_g, p_rnn, p_attn, w_out, ln1_g, ln1_b,
           w_router, router_bias, we_gate, we_up, we_down, ws_gate, ws_up, ws_down, ln2_g, ln2_b):
    bsz, seq, d = x.shape
    n = bsz * seq
    x2 = x.reshape(n, d)
    l = 0

    ada, w_in_bf, p_rnn_bf, p_attn_bf, w_out_bf, sg_bf, su_bf, sd_bf = _ada(
        c, w_ada[l], b_ada[l], [w_in[l], p_rnn[l], p_attn[l], w_out[l], ws_gate[l], ws_up[l], ws_down[l]])
    sh1, sc1, g1, sh2, sc2, g2 = [ada[:, i * d:(i + 1) * d].reshape(bsz, 1, d) for i in range(6)]

    proj = _inproj(x2, sh1, sc1, w_in_bf, seq, tm=TILES["inproj_rows"])

    ma = _rnn(proj, conv_w[l], conv_b[l].reshape(1, d),
              _block_diag_tiles(lru_w_a[l]).astype(BF16), lru_b_a[l].reshape(1, d),
              _block_diag_tiles(lru_w_i[l]).astype(BF16), lru_b_i[l].reshape(1, d),
              lru_lambda[l].reshape(1, d), p_rnn_bf, bsz, seq, tt=TILES["rnn_rows"])

    yb, wg_bf, wu_bf, wd_bf = _attn(proj, lam_q1[l].reshape(1, -1), lam_k1[l].reshape(1, -1),
                                    lam_q2[l].reshape(1, -1), lam_k2[l].reshape(1, -1),
                                    subln_g[l].reshape(1, -1), [we_gate[l], we_up[l], we_down[l]],
                                    bsz, seq, tq=TILES["attn_q_rows"])

    wr_hi = w_router[l].astype(BF16)
    wr_lo = (w_router[l] - wr_hi.astype(F32)).astype(BF16)
    x1, u, logits = _postmix(yb, ma, proj, x2, g1, sh2, sc2, ln1_g[l].reshape(1, d),
                             ln1_b[l].reshape(1, d), p_attn_bf,
                             w_out_bf, jnp.concatenate([wr_hi, wr_lo], axis=1), seq,
                             tm=TILES["postmix_rows"])

    gate = _route(logits, router_bias[l], tn=TILES["route_tokens"])

    out = _moe(u, gate, x1, g2, wg_bf, wu_bf, wd_bf,
               sg_bf, su_bf, sd_bf,
               ln2_g[l].reshape(1, d), ln2_b[l].reshape(1, d),
               seq, tm=TILES["moe_rows"], eg=TILES["moe_experts"])
    return out.reshape(bsz, seq, d)
```

```python
import functools
import math

import jax
import jax.numpy as jnp
from jax import lax
from jax.experimental import pallas as pl
from jax.experimental.pallas import tpu as pltpu

F32 = jnp.float32
BF16 = jnp.bfloat16

D_MODEL = 1024
CHUNK = 64
CONV_W = 4
LRU_C = 8.0
N_HEADS = 8
DIFF_DH = 64
HEAD_W = 2 * DIFF_DH
N_EXPERTS = 64
TOP_K = 8
N_GROUPS = 8
GROUP_SIZE = N_EXPERTS // N_GROUPS
TOPK_GROUPS = 4
ROUTED_SCALE = 2.5
DEPTH = 1
ALPHA = (2.0 * DEPTH) ** 0.25
LN_EPS = 1e-5
RMS_EPS = 1e-5
LAM_INIT = 0.8 - 0.6 * math.exp(-0.3 * 0)

COL_X, COL_Y, COL_Q, COL_K, COL_V, COL_GR, COL_GA = range(7)
N_COLS = 7

V7X_VMEM_LIMIT = 56 * 1024 * 1024
MXU_TILE = 256
SUBLANES = 8
LANES = 128
N_RNN_IN = 11
RNN_PHASE_AT = (2, 4)
ADA_STEPS = 8
POSTMIX_CHUNK = 512

TILES = dict(
    inproj_rows=1024,
    attn_q_rows=256,
    postmix_rows=1024,
    route_tokens=2048,
    moe_rows=1024,
    moe_experts=5,
)


def _ln0(x):
    mu = jnp.mean(x, axis=-1, keepdims=True)
    xc = x - mu
    var = jnp.mean(xc * xc, axis=-1, keepdims=True)
    return xc * lax.rsqrt(var + LN_EPS)


def _silu(x):
    return x * jax.nn.sigmoid(x)


def _params(sem):
    return pltpu.CompilerParams(dimension_semantics=sem, vmem_limit_bytes=V7X_VMEM_LIMIT)


def _ada_kernel(c_ref, w_ref, b_ref, *rest):
    n_cast = len(rest) // 2
    o_ref = rest[n_cast]
    for src, dst in zip(rest[:n_cast], rest[n_cast + 1:]):
        dst[...] = src[...].astype(BF16)
    cond = _silu(c_ref[...])
    o_ref[...] = jnp.dot(cond, w_ref[...], preferred_element_type=F32,
                         precision=lax.Precision.HIGHEST) + b_ref[...]


def _ada(c, w, b, to_bf16):
    bsz, d = c.shape
    n_out = w.shape[1]
    steps = ADA_STEPS
    tn = n_out // steps

    def slab(a):
        assert a.shape[0] % (steps * 2 * SUBLANES) == 0
        return pl.BlockSpec((a.shape[0] // steps, a.shape[1]), lambda j: (j, 0))

    return pl.pallas_call(
        _ada_kernel,
        grid=(steps,),
        in_specs=[pl.BlockSpec((bsz, d), lambda j: (0, 0)),
                  pl.BlockSpec((d, tn), lambda j: (0, j)),
                  pl.BlockSpec((1, tn), lambda j: (0, j)),
                  *[slab(a) for a in to_bf16]],
        out_specs=[pl.BlockSpec((bsz, tn), lambda j: (0, j)), *[slab(a) for a in to_bf16]],
        out_shape=[jax.ShapeDtypeStruct((bsz, n_out), F32),
                   *[jax.ShapeDtypeStruct(a.shape, BF16) for a in to_bf16]],
        compiler_params=_params(("parallel",)),
        name="ada",
    )(c, w, b.reshape(1, n_out), *to_bf16)


def _inproj_kernel(x_ref, sh_ref, sc_ref, w_ref, o_ref):
    d = x_ref.shape[1]
    h = (_ln0(x_ref[...]) * (1.0 + sc_ref[0]) + sh_ref[0]).astype(BF16)
    for j in range(N_COLS):
        cols = slice(j * d, (j + 1) * d)
        acc = jnp.dot(h, w_ref[:, cols], preferred_element_type=F32)
        if j >= COL_GR:
            acc = jax.nn.sigmoid(acc)
        o_ref[:, cols] = acc.astype(BF16)


def _inproj(x2, sh, sc, w, seq, tm):
    n, d = x2.shape
    per_b = seq // tm
    return pl.pallas_call(
        _inproj_kernel,
        grid=(n // tm,),
        in_specs=[pl.BlockSpec((tm, d), lambda i: (i, 0)),
                  pl.BlockSpec((1, 1, d), lambda i: (i // per_b, 0, 0)),
                  pl.BlockSpec((1, 1, d), lambda i: (i // per_b, 0, 0)),
                  pl.BlockSpec(w.shape, lambda i: (0, 0))],
        out_specs=pl.BlockSpec((tm, N_COLS * d), lambda i: (i, 0)),
        out_shape=jax.ShapeDtypeStruct((n, N_COLS * d), BF16),
        compiler_params=_params(("parallel",)),
        name="inproj",
    )(x2, sh, sc, w)


def _stream_pitch(n_slab):
    assert n_slab % SUBLANES == 0
    return n_slab if (n_slab // SUBLANES) % 2 else n_slab + SUBLANES


def _rnn_tile(first, xr_ref, yr_ref, g_ref, cw_ref, cb_ref, wa_ref, ba_ref, wi_ref, bi_ref,
              lam_ref, p_ref, o_ref, x_scr, h_stage, tail_scr, h_scr):
    tt, d = xr_ref.shape
    n_slab = tt // SUBLANES
    n_hist = CONV_W - 1

    @pl.when(first)
    def _():
        tail_scr[...] = jnp.zeros_like(tail_scr)
        h_scr[...] = jnp.zeros_like(h_scr)

    def slab(v, j):
        return v[j * SUBLANES:(j + 1) * SUBLANES, :]

    n_lane = d // LANES
    pitch = _stream_pitch(n_slab)

    def permuted(src_ref, scr):
        x = src_ref[...].astype(F32)
        for c in range(n_lane):
            for s in range(SUBLANES):
                scr[c, s * pitch:s * pitch + n_slab, :] = x[s * n_slab:(s + 1) * n_slab, c * LANES:(c + 1) * LANES]
        return [jnp.concatenate([scr[c, pl.ds(j, SUBLANES, stride=pitch), :] for c in range(n_lane)],
                                axis=-1) for j in range(n_slab)]

    sub = lax.broadcasted_iota(jnp.int32, (SUBLANES, d), 0)

    xs = permuted(xr_ref, x_scr)
    wrapped = []
    for i in range(n_hist):
        cur = xs[n_slab - n_hist + i]
        prev = slab(tail_scr, i)
        wrapped.append(pltpu.roll(jnp.where(sub == SUBLANES - 1, prev, cur), 1, 0))
        tail_scr[i * SUBLANES:(i + 1) * SUBLANES, :] = cur

    def x_at(j):
        return xs[j] if j >= 0 else wrapped[j + n_hist]

    cw = cw_ref[...]
    w = [cw[i:i + 1, :] for i in range(CONV_W)]
    cb = cb_ref[...]
    xc = jnp.concatenate(
        [sum((w[CONV_W - 1 - k] * x_at(j - k) for k in range(CONV_W)), start=cb) for j in range(n_slab)],
        axis=0)

    xcb = xc.astype(BF16)
    n_g = d // MXU_TILE
    ra = jnp.concatenate(
        [jnp.dot(xcb[:, g * MXU_TILE:(g + 1) * MXU_TILE], wa_ref[g], preferred_element_type=F32)
         for g in range(n_g)], axis=-1)
    ri = jnp.concatenate(
        [jnp.dot(xcb[:, g * MXU_TILE:(g + 1) * MXU_TILE], wi_ref[g], preferred_element_type=F32)
         for g in range(n_g)], axis=-1)
    yield
    r = jax.nn.sigmoid(ra + ba_ref[...])
    gi = jax.nn.sigmoid(ri + bi_ref[...])
    z = -lam_ref[...]
    softplus = jnp.maximum(z, 0.0) + jnp.log1p(jnp.exp(-jnp.abs(z)))
    log_a = (-LRU_C) * r * softplus
    a = jnp.exp(log_a)
    mult = jnp.sqrt(jnp.maximum(1.0 - a * a, 0.0))
    u = xc * gi * mult

    h_loc, p_loc = [slab(u, 0)], [slab(a, 0)]
    for j in range(1, n_slab):
        aj = slab(a, j)
        h_loc.append(aj * h_loc[-1] + slab(u, j))
        p_loc.append(aj * p_loc[-1])
    pa, hb = p_loc[-1], h_loc[-1]
    step = 1
    while step < SUBLANES:
        valid = sub >= step
        hb = jnp.where(valid, pa * pltpu.roll(hb, step, 0) + hb, hb)
        pa = jnp.where(valid, pa * pltpu.roll(pa, step, 0), pa)
        step *= 2
    carry = h_scr[...]
    end = pa * carry + hb
    start = jnp.where(sub == 0, carry, pltpu.roll(end, 1, 0))
    h_scr[...] = end[SUBLANES - 1:SUBLANES, :]
    for j in range(n_slab):
        hj = h_loc[j] + p_loc[j] * start
        for c in range(n_lane):
            h_stage[c, pl.ds(j, SUBLANES, stride=pitch), :] = hj[:, c * LANES:(c + 1) * LANES]
    h = jnp.concatenate(
        [jnp.concatenate([h_stage[c, s * pitch:s * pitch + n_slab, :] for s in range(SUBLANES)], axis=0)
         for c in range(n_lane)], axis=-1)
    ya = (h * jax.nn.gelu(yr_ref[...].astype(F32))).astype(BF16)
    yield
    out = g_ref[...].astype(F32) * jnp.dot(ya, p_ref[...], preferred_element_type=F32)
    o_ref[...] = out.astype(BF16)
    yield


def _attn_kernel(q_ref, k_ref, v_ref, lq1_ref, lk1_ref, lq2_ref, lk2_ref, sg_ref, *rest, tq, n_cast):
    cast_in, rnn_in = rest[:n_cast], rest[n_cast:n_cast + N_RNN_IN]
    o_ref = rest[n_cast + N_RNN_IN]
    cast_out = rest[n_cast + N_RNN_IN + 1:2 * n_cast + N_RNN_IN + 1]
    ma_ref, rnn_scr = rest[2 * n_cast + N_RNN_IN + 1], rest[2 * n_cast + N_RNN_IN + 2:]
    for src, dst in zip(cast_in, cast_out):
        dst[...] = src[...].astype(BF16)
    rnn = _rnn_tile(pl.program_id(1) == 0, *rnn_in, ma_ref, *rnn_scr)
    next(rnn)
    seq = k_ref.shape[0]
    lam = (jnp.exp(jnp.sum(lq1_ref[...] * lk1_ref[...], axis=-1, keepdims=True))
           - jnp.exp(jnp.sum(lq2_ref[...] * lk2_ref[...], axis=-1, keepdims=True)) + LAM_INIT)
    nt = (((1,), (1,)), ((), ()))
    scale = DIFF_DH ** -0.5
    lane = lax.broadcasted_iota(jnp.int32, (tq, HEAD_W), 1)
    diag = (lax.broadcasted_iota(jnp.int32, (tq, tq), 1) // CHUNK
            <= lax.broadcasted_iota(jnp.int32, (tq, tq), 0) // CHUNK)

    diag2 = jnp.concatenate([diag, diag], axis=0)

    for idx, qi in enumerate(reversed(range(seq // tq))):
        if idx in RNN_PHASE_AT:
            next(rnn)
        kv = (qi + 1) * tq
        q = q_ref[qi * tq:(qi + 1) * tq, :] * scale
        zero = jnp.zeros_like(q)
        qq = jnp.concatenate([jnp.where(lane < DIFF_DH, q, zero), jnp.where(lane >= DIFF_DH, q, zero)], axis=0)
        s = lax.dot_general(qq, k_ref[0:kv, :], nt, preferred_element_type=F32)
        s_diag = jnp.where(diag2, s[:, kv - tq:], -jnp.inf)
        s = s_diag if kv == tq else jnp.concatenate([s[:, :kv - tq], s_diag], axis=1)
        e = jnp.exp(s - jnp.max(s, axis=-1, keepdims=True))
        pv = jnp.dot(e.astype(BF16), v_ref[0:kv, :], preferred_element_type=F32)
        pv = pv / jnp.sum(e, axis=-1, keepdims=True)
        o = pv[:tq] - lam * pv[tq:]
        o = o * lax.rsqrt(jnp.mean(o * o, axis=-1, keepdims=True) + RMS_EPS) * sg_ref[...]
        o_ref[qi * tq:(qi + 1) * tq, :] = (o * (1.0 - LAM_INIT)).astype(BF16)


def _attn(proj, lq1, lk1, lq2, lk2, sg, rnn_params, to_bf16, bsz, seq, tq):
    n = proj.shape[0]
    d = D_MODEL
    steps = bsz * N_HEADS
    tt = seq // N_HEADS

    def slab(a):
        assert a.shape[0] % steps == 0
        return pl.BlockSpec((a.shape[0] // steps,) + a.shape[1:], lambda b, h: (b * N_HEADS + h, 0, 0))

    cw, cb, wa, ba, wi, bi, lam, p_rnn = rnn_params
    hb = D_MODEL // HEAD_W
    lam_spec = pl.BlockSpec((1, DIFF_DH), lambda b, h: (0, 0))
    vec = pl.BlockSpec((1, d), lambda b, h: (0, 0))
    gate_w = pl.BlockSpec(wa.shape, lambda b, h: (0, 0, 0))
    rrow = lambda col: pl.BlockSpec((tt, d), lambda b, h: (b * N_HEADS + h, col))
    tile = pltpu.VMEM((d // LANES, SUBLANES * _stream_pitch(tt // SUBLANES), LANES), F32)
    rnn_in = [proj, proj, proj, cw, cb, wa, ba, wi, bi, lam, p_rnn]
    assert len(rnn_in) == N_RNN_IN
    return pl.pallas_call(
        functools.partial(_attn_kernel, tq=tq, n_cast=len(to_bf16)),
        grid=(bsz, N_HEADS),
        in_specs=[pl.BlockSpec((seq, HEAD_W), lambda b, h: (b, COL_Q * hb + h)),
                  pl.BlockSpec((seq, HEAD_W), lambda b, h: (b, COL_K * hb + h)),
                  pl.BlockSpec((seq, HEAD_W), lambda b, h: (b, COL_V * hb + h)),
                  lam_spec, lam_spec, lam_spec, lam_spec,
                  pl.BlockSpec((1, HEAD_W), lambda b, h: (0, 0)),
                  *[slab(a) for a in to_bf16],
                  rrow(COL_X), rrow(COL_Y), rrow(COL_GR),
                  pl.BlockSpec((CONV_W, d), lambda b, h: (0, 0)),
                  vec, gate_w, vec, gate_w, vec, vec,
                  pl.BlockSpec((d, d), lambda b, h: (0, 0))],
        out_specs=[pl.BlockSpec((seq, HEAD_W), lambda b, h: (b, h)), *[slab(a) for a in to_bf16],
                   pl.BlockSpec((tt, d), lambda b, h: (b * N_HEADS + h, 0))],
        out_shape=[jax.ShapeDtypeStruct((n, D_MODEL), BF16),
                   *[jax.ShapeDtypeStruct(a.shape, BF16) for a in to_bf16],
                   jax.ShapeDtypeStruct((n, d), BF16)],
        scratch_shapes=[tile, tile,
                        pltpu.VMEM(((CONV_W - 1) * SUBLANES, d), F32), pltpu.VMEM((1, d), F32)],
        compiler_params=_params(("parallel", "arbitrary")),
        name="attn",
    )(proj, proj, proj, lq1, lk1, lq2, lk2, sg, *to_bf16, *rnn_in)


def _postmix_kernel(yb_ref, ma_ref, ga_ref, x_ref, g1_ref, sh_ref, sc_ref, lg_ref, lb_ref,
                    pa_ref, wo_ref, wr_ref, x1_ref, u_ref, lt_ref):
    tm = x_ref.shape[0]

    def project(rows):
        branch_b = jnp.dot(yb_ref[rows, :], pa_ref[...], preferred_element_type=F32)
        merged = ma_ref[rows, :].astype(F32) + ga_ref[rows, :].astype(F32) * branch_b
        return jnp.dot(merged.astype(BF16), wo_ref[...], preferred_element_type=F32)

    def normalize(rows, mix):
        x1 = _ln0(ALPHA * x_ref[rows, :] + g1_ref[0] * mix) * lg_ref[...] + lb_ref[...]
        x1_ref[rows, :] = x1
        u = _ln0(x1) * (1.0 + sc_ref[0]) + sh_ref[0]
        u_ref[rows, :] = u.astype(BF16)
        u_hi = u.astype(BF16)
        u_lo = (u - u_hi.astype(F32)).astype(BF16)
        a = jnp.dot(u_hi, wr_ref[...], preferred_element_type=F32)
        b = jnp.dot(u_lo, wr_ref[...], preferred_element_type=F32)
        lt_ref[rows, :] = a + pltpu.roll(a, N_EXPERTS, 1) + b

    chunks = [slice(r0, r0 + POSTMIX_CHUNK) for r0 in range(0, tm, POSTMIX_CHUNK)]
    mix = project(chunks[0])
    for prev, rows in zip(chunks[:-1], chunks[1:]):
        nxt = project(rows)
        normalize(prev, mix)
        mix = nxt
    normalize(chunks[-1], mix)


def _postmix(yb, ma, proj, x2, g1, sh2, sc2, ln_g, ln_b, p_attn, w_out, wr_hilo, seq, tm):
    n, d = x2.shape
    per_b = seq // tm
    row = pl.BlockSpec((tm, d), lambda i: (i, 0))
    mod = pl.BlockSpec((1, 1, d), lambda i: (i // per_b, 0, 0))
    vec = pl.BlockSpec((1, d), lambda i: (0, 0))
    mat = pl.BlockSpec((d, d), lambda i: (0, 0))
    return pl.pallas_call(
        _postmix_kernel,
        grid=(n // tm,),
        in_specs=[row, row, pl.BlockSpec((tm, d), lambda i: (i, COL_GA)), row,
                  mod, mod, mod, vec, vec, mat, mat,
                  pl.BlockSpec((d, 2 * N_EXPERTS), lambda i: (0, 0))],
        out_specs=[row, row, pl.BlockSpec((tm, 2 * N_EXPERTS), lambda i: (i, 0))],
        out_shape=[jax.ShapeDtypeStruct((n, d), F32),
                   jax.ShapeDtypeStruct((n, d), BF16),
                   jax.ShapeDtypeStruct((n, 2 * N_EXPERTS), F32)],
        compiler_params=_params(("parallel",)),
        name="postmix",
    )(yb, ma, proj, x2, g1, sh2, sc2, ln_g, ln_b, p_attn, w_out, wr_hilo)


def _route_kernel(lt_ref, bias_ref, gate_ref):
    tn = lt_ref.shape[0]
    shape3 = (N_GROUPS, GROUP_SIZE, tn)
    neg = -jnp.inf
    s = jax.nn.sigmoid(lt_ref[...].T[:N_EXPERTS, :])
    sel = (s + bias_ref[...]).reshape(shape3)
    s = s.reshape(shape3)
    j_idx = lax.broadcasted_iota(jnp.int32, shape3, 1).astype(F32)
    g_idx = lax.broadcasted_iota(jnp.int32, shape3, 0).astype(F32)
    e_idx = g_idx * GROUP_SIZE + j_idx

    m1 = jnp.max(sel, axis=1, keepdims=True)
    first = jnp.min(jnp.where(sel == m1, j_idx, float(GROUP_SIZE)), axis=1, keepdims=True)
    m2 = jnp.max(jnp.where(j_idx == first, neg, sel), axis=1, keepdims=True)
    gscore = m1 + m2

    gi = lax.broadcasted_iota(jnp.int32, gscore.shape, 0).astype(F32)
    gkeep = jnp.zeros(gscore.shape, jnp.bool_)
    for _ in range(TOPK_GROUPS):
        m = jnp.max(gscore, axis=0, keepdims=True)
        pick = jnp.min(jnp.where(gscore == m, gi, float(N_GROUPS)), axis=0, keepdims=True)
        hit = gi == pick
        gkeep = jnp.logical_or(gkeep, hit)
        gscore = jnp.where(hit, neg, gscore)
    sel = jnp.where(gkeep, sel, neg)

    picked = jnp.zeros(shape3, F32)
    for _ in range(TOP_K):
        m = jnp.max(jnp.max(sel, axis=1, keepdims=True), axis=0, keepdims=True)
        cand = jnp.where(sel == m, e_idx, float(N_EXPERTS))
        pick = jnp.min(jnp.min(cand, axis=1, keepdims=True), axis=0, keepdims=True)
        hit = e_idx == pick
        picked = jnp.where(hit, s, picked)
        sel = jnp.where(hit, neg, sel)
    total = jnp.sum(jnp.sum(picked, axis=1, keepdims=True), axis=0, keepdims=True)
    gate = picked / total * ROUTED_SCALE
    gate = gate.reshape(N_EXPERTS, tn)
    shared = (lax.broadcasted_iota(jnp.int32, gate.shape, 0) == 0).astype(F32)
    gate_ref[...] = jnp.concatenate([gate, shared], axis=0).T


def _route(logits, bias, tn):
    n, w = logits.shape
    e = bias.shape[0]
    return pl.pallas_call(
        _route_kernel,
        grid=(n // tn,),
        in_specs=[pl.BlockSpec((tn, w), lambda i: (i, 0)),
                  pl.BlockSpec((e, 1), lambda i: (0, 0))],
        out_specs=pl.BlockSpec((tn, w), lambda i: (i, 0)),
        out_shape=jax.ShapeDtypeStruct((n, w), F32),
        compiler_params=_params(("parallel",)),
        name="route",
    )(logits, bias.reshape(e, 1))


def _moe_kernel(*refs, eg):
    u_ref, gate_ref, x1_ref, g2_ref = refs[:4]
    wg_refs, wu_refs, wd_refs = refs[4:4 + eg], refs[4 + eg:4 + 2 * eg], refs[4 + 2 * eg:4 + 3 * eg]
    sg_ref, su_ref, sd_ref, lg_ref, lb_ref, o_ref, acc_scr = refs[4 + 3 * eg:]
    j = pl.program_id(1)
    last = j == pl.num_programs(1) - 1

    @pl.when(j == 0)
    def _():
        acc_scr[...] = jnp.zeros_like(acc_scr)

    u = u_ref[...]
    gate = gate_ref[...]
    lane = lax.broadcasted_iota(jnp.int32, gate.shape, 1)
    acts, downs = [], []
    for el in range(eg):
        wg, wu, wd = wg_refs[el][0], wu_refs[el][0], wd_refs[el][0]
        if el == eg - 1:
            wg = jnp.where(last, sg_ref[...], wg)
            wu = jnp.where(last, su_ref[...], wu)
            wd = jnp.where(last, sd_ref[...], wd)
        gk = jnp.sum(jnp.where(lane == j * eg + el, gate, 0.0), axis=-1, keepdims=True)
        hg = jnp.dot(u, wg, preferred_element_type=F32)
        hu = jnp.dot(u, wu, preferred_element_type=F32)
        acts.append((_silu(hg) * hu * gk).astype(BF16))
        downs.append(wd)
    acc_scr[...] += jnp.dot(jnp.concatenate(acts, axis=-1), jnp.concatenate(downs, axis=0),
                            preferred_element_type=F32)

    @pl.when(last)
    def _():
        z = ALPHA * x1_ref[...] + g2_ref[0] * acc_scr[...]
        o_ref[...] = _ln0(z) * lg_ref[...] + lb_ref[...]


def _moe(u, gate, x1, g2, wg, wu, wd, sg, su, sd, ln_g, ln_b, seq, tm, eg):
    n, d = x1.shape
    n_e, _, f = wg.shape
    assert (n_e + 1) % eg == 0 and n_e < gate.shape[1] and sg.shape == (d, f)
    per_b = seq // tm
    row = pl.BlockSpec((tm, d), lambda i, j: (i, 0))
    vec = pl.BlockSpec((1, d), lambda i, j: (0, 0))
    const = lambda a: pl.BlockSpec(a.shape, lambda i, j: (0, 0))

    def expert(shape, el):
        return pl.BlockSpec((1,) + shape, lambda i, j: (jnp.minimum(j * eg + el, n_e - 1), 0, 0))

    slots = range(eg)
    return pl.pallas_call(
        functools.partial(_moe_kernel, eg=eg),
        grid=(n // tm, (n_e + 1) // eg),
        in_specs=[row, pl.BlockSpec((tm, gate.shape[1]), lambda i, j: (i, 0)), row,
                  pl.BlockSpec((1, 1, d), lambda i, j: (i // per_b, 0, 0)),
                  *[expert((d, f), el) for el in slots], *[expert((d, f), el) for el in slots],
                  *[expert((f, d), el) for el in slots],
                  const(sg), const(su), const(sd), vec, vec],
        out_specs=row,
        out_shape=jax.ShapeDtypeStruct((n, d), F32),
        scratch_shapes=[pltpu.VMEM((tm, d), F32)],
        compiler_params=_params(("parallel", "arbitrary")),
        name="moe",
    )(u, gate, x1, g2, *[wg] * eg, *[wu] * eg, *[wd] * eg, sg, su, sd, ln_g, ln_b)


def _block_diag_tiles(w):
    nb, bw, _ = w.shape
    per = MXU_TILE // bw
    w = w.reshape(nb // per, per, bw, bw)
    eye = jnp.eye(per, dtype=w.dtype)
    tiles = jnp.einsum("gpij,pq->gpiqj", w, eye)
    return tiles.reshape(nb // per, MXU_TILE, MXU_TILE)


def kernel(x, c, w_ada, b_ada, w_in, conv_w, conv_b, lru_w_a, lru_b_a, lru_w_i, lru_b_i, lru_lambda,
           lam_q1, lam_k1, lam_q2, lam_k2, subln_g, p_rnn, p_attn, w_out, ln1_g, ln1_b,
           w_router, router_bias, we_gate, we_up, we_down, ws_gate, ws_up, ws_down, ln2_g, ln2_b):
    bsz, seq, d = x.shape
    n = bsz * seq
    x2 = x.reshape(n, d)
    l = 0

    ada, w_in_bf, p_rnn_bf, p_attn_bf, w_out_bf, sg_bf, su_bf, sd_bf = _ada(
        c, w_ada[l], b_ada[l], [w_in[l], p_rnn[l], p_attn[l], w_out[l], ws_gate[l], ws_up[l], ws_down[l]])
    sh1, sc1, g1, sh2, sc2, g2 = [ada[:, i * d:(i + 1) * d].reshape(bsz, 1, d) for i in range(6)]

    proj = _inproj(x2, sh1, sc1, w_in_bf, seq, tm=TILES["inproj_rows"])

    rnn_params = (conv_w[l], conv_b[l].reshape(1, d),
                  _block_diag_tiles(lru_w_a[l]).astype(BF16), lru_b_a[l].reshape(1, d),
                  _block_diag_tiles(lru_w_i[l]).astype(BF16), lru_b_i[l].reshape(1, d),
                  lru_lambda[l].reshape(1, d), p_rnn_bf)
    yb, wg_bf, wu_bf, wd_bf, ma = _attn(proj, lam_q1[l].reshape(1, -1), lam_k1[l].reshape(1, -1),
                                        lam_q2[l].reshape(1, -1), lam_k2[l].reshape(1, -1),
                                        subln_g[l].reshape(1, -1), rnn_params,
                                        [we_gate[l], we_up[l], we_down[l]],
                                        bsz, seq, tq=TILES["attn_q_rows"])

    wr_hi = w_router[l].astype(BF16)
    wr_lo = (w_router[l] - wr_hi.astype(F32)).astype(BF16)
    x1, u, logits = _postmix(yb, ma, proj, x2, g1, sh2, sc2, ln1_g[l].reshape(1, d),
                             ln1_b[l].reshape(1, d), p_attn_bf,
                             w_out_bf, jnp.concatenate([wr_hi, wr_lo], axis=1), seq,
                             tm=TILES["postmix_rows"])

    gate = _route(logits, router_bias[l], tn=TILES["route_tokens"])

    out = _moe(u, gate, x1, g2, wg_bf, wu_bf, wd_bf,
               sg_bf, su_bf, sd_bf,
               ln2_g[l].reshape(1, d), ln2_b[l].reshape(1, d),
               seq, tm=TILES["moe_rows"], eg=TILES["moe_experts"])
    return out.reshape(bsz, seq, d)
```
